```python
import math
import jax, jax.numpy as jnp
from jax import lax
import numpy as np

D_MODEL = 1024
BATCH = 2
SEQ = 16384
DEPTH = 2
DEC_BATCH = 8
DEC_SEQ = 64
PAST_LEN = 2048

CHUNK = 64
N_A = DEPTH // 2
N_B = DEPTH - N_A
RET_DK = 256
RET_HEADS = D_MODEL // RET_DK
RET_DV = 2 * RET_DK
FOX_DH = 64
FOX_HEADS = D_MODEL // FOX_DH
D_FF = ((8 * D_MODEL // 3 + 127) // 128) * 128
CONV_W = 3
Q_BLOCK = 128
ROPE_BASE = 10000.0
LN_EPS = 1e-5
ALPHA = (2 * DEPTH) ** 0.25
BETA = (8 * DEPTH) ** -0.25
NEG_INF = -1e30

kernel_name = 'yoco_retention_fox_convffn_step'


def _layernorm(x, g, b):
    xf = x.astype(jnp.float32)
    mu = jnp.mean(xf, axis=-1, keepdims=True)
    var = jnp.mean(jnp.square(xf - mu), axis=-1, keepdims=True)
    y = (xf - mu) * lax.rsqrt(var + LN_EPS)
    return (y * g.astype(jnp.float32) + b.astype(jnp.float32)).astype(x.dtype)


def _rope(x, pos):
    half = x.shape[-1] // 2
    inv = 1.0 / (ROPE_BASE ** (jnp.arange(half, dtype=jnp.float32) / half))
    ang = pos.astype(jnp.float32)[:, None] * inv[None, :]
    cos = jnp.cos(ang)[None, :, None, :].astype(x.dtype)
    sin = jnp.sin(ang)[None, :, None, :].astype(x.dtype)
    x1, x2 = x[..., :half], x[..., half:]
    return jnp.concatenate([x1 * cos - x2 * sin, x2 * cos + x1 * sin], axis=-1)


def _ret_log_gamma():
    return jnp.log1p(-jnp.exp2(-5.0 - jnp.arange(RET_HEADS, dtype=jnp.float32)))


def _retention_chunk(S, qkv):
    q, k, v = qkv
    C = q.shape[1]
    dt = q.dtype
    lg = _ret_log_gamma()
    idx = jnp.arange(C, dtype=jnp.float32)
    diff = idx[:, None] - idx[None, :]
    dmask = jnp.where(diff >= 0, jnp.exp(lg[:, None, None] * jnp.maximum(diff, 0.0)), 0.0).astype(dt)
    scores = jnp.einsum('bihd,bjhd->bhij', q, k) * dmask[None]
    inner = jnp.einsum('bhij,bjhe->bihe', scores, v)
    q_decay = jnp.exp((idx[:, None] + 1.0) * lg[None, :]).astype(dt)
    cross = jnp.einsum('bihd,bhde->bihe', q * q_decay[None, :, :, None], S)
    k_decay = jnp.exp((C - 1.0 - idx)[:, None] * lg[None, :]).astype(dt)
    S_new = (jnp.exp(C * lg).astype(dt)[None, :, None, None] * S
             + jnp.einsum('bjhd,bjhe->bhde', k * k_decay[None, :, :, None], v))
    return S_new, inner + cross


def _retention(q, k, v, S0):
    B, L = q.shape[0], q.shape[1]
    if L <= CHUNK:
        S, o = _retention_chunk(S0, (q, k, v))
        return o, S
    nc = L // CHUNK

    def to_chunks(t):
        return t.reshape(B, nc, CHUNK, *t.shape[2:]).swapaxes(0, 1)

    S, o = lax.scan(_retention_chunk, S0, (to_chunks(q), to_chunks(k), to_chunks(v)))
    o = o.swapaxes(0, 1).reshape(B, L, *o.shape[3:])
    return o, S


def _fox_block(qb, qpos_b, cq_b, k, v, ck, kpos):
    s = jnp.einsum('bqhd,bkhd->bhqk', qb, k).astype(jnp.float32) * (FOX_DH ** -0.5)
    s = s + (jnp.swapaxes(cq_b, 1, 2)[:, :, :, None] - jnp.swapaxes(ck, 1, 2)[:, :, None, :])
    mask = kpos[None, :] <= qpos_b[:, None]
    s = jnp.where(mask[None, None], s, NEG_INF)
    p = jax.nn.softmax(s, axis=-1).astype(v.dtype)
    return jnp.einsum('bhqk,bkhd->bqhd', p, v)


def _forgetting_attention(q, k, v, c, q_pos, k_pos):
    B, Lq = q.shape[0], q.shape[1]
    cq = c[:, -Lq:]
    if Lq <= Q_BLOCK:
        return _fox_block(q, q_pos, cq, k, v, c, k_pos)
    nb = Lq // Q_BLOCK
    qb = q.reshape(B, nb, Q_BLOCK, *q.shape[2:]).swapaxes(0, 1)
    cqb = cq.reshape(B, nb, Q_BLOCK, cq.shape[-1]).swapaxes(0, 1)
    pb = q_pos.reshape(nb, Q_BLOCK)
    o = lax.map(lambda a: _fox_block(a[0], a[1], a[2], k, v, c, k_pos), (qb, pb, cqb))
    return o.swapaxes(0, 1).reshape(q.shape)


def _conv_ffn(x, conv_state, w_up, conv_w, conv_b, w_down):
    L = x.shape[1]
    h = x @ w_up
    val, a = h[..., :D_FF], h[..., D_FF:]
    a_ext = jnp.concatenate([conv_state.astype(a.dtype), a], axis=1)
    conv = conv_b
    for j in range(CONV_W):
        conv = conv + conv_w[j] * a_ext[:, j:j + L]
    hidden = jax.nn.gelu(conv, approximate=False) * val
    return hidden @ w_down, a_ext[:, -(CONV_W - 1):]


def _trunk(x, pos0, ret_state0, conv_state0, k_past, v_past, logf_past,
           w_in_a, ln_ret_g, ln_ret_b, w_out_a, w_kvf, b_f, w_q_b, w_out_b,
           ln_mix_g, ln_mix_b, w_up, conv_w, conv_b, w_down, ln_ffn_g, ln_ffn_b):
    B, L, _ = x.shape
    pos = pos0 + jnp.arange(L, dtype=jnp.int32)
    HK, HV, HB = RET_HEADS * RET_DK, RET_HEADS * RET_DV, FOX_HEADS * FOX_DH
    ret_states, conv_states = [], []
    k_new = v_new = logf_new = None
    k_all = v_all = c_all = k_pos = None
    for layer in range(DEPTH):
        if layer < N_A:
            proj = x @ w_in_a[layer]
            q = _rope(proj[..., :HK].reshape(B, L, RET_HEADS, RET_DK), pos)
            k = _rope(proj[..., HK:2 * HK].reshape(B, L, RET_HEADS, RET_DK), pos) * (RET_DK ** -0.5)
            v = proj[..., 2 * HK:2 * HK + HV].reshape(B, L, RET_HEADS, RET_DV)
            g = proj[..., 2 * HK + HV:]
            o, S = _retention(q, k, v, ret_state0[layer].astype(q.dtype))
            ret_states.append(S)
            o = _layernorm(o, ln_ret_g[layer].reshape(RET_HEADS, RET_DV),
                           ln_ret_b[layer].reshape(RET_HEADS, RET_DV)).reshape(B, L, HV)
            mix = (jax.nn.silu(g) * o) @ w_out_a[layer]
        else:
            jb = layer - N_A
            if jb == 0:
                kvf = x @ w_kvf
                k_new = kvf[..., :HB].reshape(B, L, FOX_HEADS, FOX_DH)
                v_new = kvf[..., HB:2 * HB].reshape(B, L, FOX_HEADS, FOX_DH)
                logf32 = jax.nn.log_sigmoid((kvf[..., 2 * HB:] + b_f).astype(jnp.float32))
                logf_new = logf32.astype(x.dtype)
                k_all = jnp.concatenate([k_past.astype(x.dtype), k_new], axis=1)
                v_all = jnp.concatenate([v_past.astype(x.dtype), v_new], axis=1)
                c_all = jnp.cumsum(jnp.concatenate([logf_past.astype(jnp.float32), logf32], axis=1), axis=1)
                k_pos = jnp.arange(k_all.shape[1], dtype=jnp.int32)
            q = (x @ w_q_b[jb]).reshape(B, L, FOX_HEADS, FOX_DH)
            o = _forgetting_attention(q, k_all, v_all, c_all, pos, k_pos)
            mix = o.reshape(B, L, HB) @ w_out_b[jb]
        x = _layernorm(ALPHA * x + mix, ln_mix_g[layer], ln_mix_b[layer])
        f, cs = _conv_ffn(x, conv_state0[layer], w_up[layer], conv_w[layer], conv_b[layer], w_down[layer])
        conv_states.append(cs)
        x = _layernorm(ALPHA * x + f, ln_ffn_g[layer], ln_ffn_b[layer])
    return x, jnp.stack(ret_states), k_new, v_new, logf_new, jnp.stack(conv_states)


def setup_inputs(seed: int = 0) -> dict:
    key = jax.random.key(seed)
    ks = jax.random.split(key, 24)
    f32 = jnp.float32

    def nrm(k, shape, scale):
        return scale * jax.random.normal(k, shape, f32)

    HK, HV, HB = RET_HEADS * RET_DK, RET_HEADS * RET_DV, FOX_HEADS * FOX_DH
    fgate_bias = jnp.linspace(1.0, 6.0, FOX_HEADS, dtype=f32)
    x_prompt = nrm(ks[0], (BATCH, SEQ, D_MODEL), 1.0)
    x_sample = nrm(ks[1], (DEC_BATCH, DEC_SEQ, D_MODEL), 1.0)
    cache_k = nrm(ks[2], (DEC_BATCH, PAST_LEN, FOX_HEADS, FOX_DH), 1.0)
    cache_v = nrm(ks[3], (DEC_BATCH, PAST_LEN, FOX_HEADS, FOX_DH), BETA)
    cache_logf = jax.nn.log_sigmoid(nrm(ks[4], (DEC_BATCH, PAST_LEN, FOX_HEADS), 1.0) + fgate_bias)
    state_ret = nrm(ks[5], (N_A, DEC_BATCH, RET_HEADS, RET_DK, RET_DV), 0.1)
    state_ffn_conv = nrm(ks[6], (DEPTH, DEC_BATCH, CONV_W - 1, D_FF), 1.0)
    col_a = jnp.concatenate([jnp.ones((2 * HK,), f32), jnp.full((HV,), BETA, f32), jnp.ones((HV,), f32)])
    w_in_a = nrm(ks[7], (N_A, D_MODEL, 2 * HK + 2 * HV), D_MODEL ** -0.5) * col_a
    ln_ret_g = 1.0 + nrm(ks[8], (N_A, HV), 0.02)
    ln_ret_b = nrm(ks[9], (N_A, HV), 0.02)
    w_out_a = nrm(ks[10], (N_A, HV, D_MODEL), BETA * HV ** -0.5)
    col_b = jnp.concatenate([jnp.ones((HB,), f32), jnp.full((HB,), BETA, f32), jnp.ones((FOX_HEADS,), f32)])
    w_kvf = nrm(ks[11], (D_MODEL, 2 * HB + FOX_HEADS), D_MODEL ** -0.5) * col_b
    b_f = fgate_bias + nrm(ks[12], (FOX_HEADS,), 0.1)
    w_q_b = nrm(ks[13], (N_B, D_MODEL, HB), D_MODEL ** -0.5)
    w_out_b = nrm(ks[14], (N_B, HB, D_MODEL), BETA * HB ** -0.5)
    ln_mix_g = 1.0 + nrm(ks[15], (DEPTH, D_MODEL), 0.02)
    ln_mix_b = nrm(ks[16], (DEPTH, D_MODEL), 0.02)
    w_up = nrm(ks[17], (DEPTH, D_MODEL, 2 * D_FF), D_MODEL ** -0.5)
    conv_w = nrm(ks[18], (DEPTH, CONV_W, D_FF), CONV_W ** -0.5)
    conv_b = nrm(ks[19], (DEPTH, D_FF), 0.02)
    w_down = nrm(ks[20], (DEPTH, D_FF, D_MODEL), BETA * D_FF ** -0.5)
    ln_ffn_g = 1.0 + nrm(ks[21], (DEPTH, D_MODEL), 0.02)
    ln_ffn_b = nrm(ks[22], (DEPTH, D_MODEL), 0.02)
    return {'x_prompt': x_prompt, 'x_sample': x_sample, 'cache_k': cache_k, 'cache_v': cache_v,
            'cache_logf': cache_logf, 'state_ret': state_ret, 'state_ffn_conv': state_ffn_conv,
            'w_in_a': w_in_a, 'ln_ret_g': ln_ret_g, 'ln_ret_b': ln_ret_b, 'w_out_a': w_out_a,
            'w_kvf': w_kvf, 'b_f': b_f, 'w_q_b': w_q_b, 'w_out_b': w_out_b,
            'ln_mix_g': ln_mix_g, 'ln_mix_b': ln_mix_b, 'w_up': w_up, 'conv_w': conv_w, 'conv_b': conv_b,
            'w_down': w_down, 'ln_ffn_g': ln_ffn_g, 'ln_ffn_b': ln_ffn_b}


def reference(x_prompt, x_sample, cache_k, cache_v, cache_logf, state_ret, state_ffn_conv,
              w_in_a, ln_ret_g, ln_ret_b, w_out_a, w_kvf, b_f, w_q_b, w_out_b,
              ln_mix_g, ln_mix_b, w_up, conv_w, conv_b, w_down, ln_ffn_g, ln_ffn_b):
    weights = (w_in_a, ln_ret_g, ln_ret_b, w_out_a, w_kvf, b_f, w_q_b, w_out_b,
               ln_mix_g, ln_mix_b, w_up, conv_w, conv_b, w_down, ln_ffn_g, ln_ffn_b)
    dt = x_prompt.dtype
    Bp = x_prompt.shape[0]
    y_p, ret_p, k_p, v_p, lf_p, conv_p = _trunk(
        x_prompt, 0,
        jnp.zeros((N_A, Bp, RET_HEADS, RET_DK, RET_DV), dt),
        jnp.zeros((DEPTH, Bp, CONV_W - 1, D_FF), dt),
        jnp.zeros((Bp, 0, FOX_HEADS, FOX_DH), dt),
        jnp.zeros((Bp, 0, FOX_HEADS, FOX_DH), dt),
        jnp.zeros((Bp, 0, FOX_HEADS), jnp.float32),
        *weights)
    y_s, ret_s, k_s, v_s, lf_s, conv_s = _trunk(
        x_sample, cache_k.shape[1], state_ret, state_ffn_conv, cache_k, cache_v, cache_logf, *weights)
    return (y_p, y_s, ret_p, k_p, v_p, lf_p, conv_p, ret_s, k_s, v_s, lf_s, conv_s)
```

```python
import functools
import math

import numpy as np
import jax
import jax.numpy as jnp
from jax import lax
from jax.experimental import pallas as pl
from jax.experimental.pallas import tpu as pltpu

F32 = jnp.float32
BF16 = jnp.bfloat16

D_MODEL = 1024
RET_DK = 256
RET_HEADS = D_MODEL // RET_DK
RET_DV = 2 * RET_DK
HK = RET_HEADS * RET_DK
HV = RET_HEADS * RET_DV
FOX_DH = 64
FOX_HEADS = D_MODEL // FOX_DH
HB = FOX_HEADS * FOX_DH
D_FF = 2816
CONV_W = 3
ROPE_BASE = 10000.0
LN_EPS = 1e-5
DEPTH = 2
ALPHA = (2 * DEPTH) ** 0.25
NEG_INF = -1e30

LANES = 128
AUG = 2 * FOX_DH
FF_CHUNK = 256
VMEM_LIMIT = 56 * 1024 * 1024

_COL_QC = (FOX_DH, FOX_DH + 1, FOX_DH + 2)
_COL_KC = (FOX_DH + 3, FOX_DH + 4, FOX_DH + 5)


def _dot(a, b):
    return jnp.dot(a, b, preferred_element_type=F32)


def _dot_nt(a, b):
    return lax.dot_general(a, b, (((1,), (1,)), ((), ())), preferred_element_type=F32)


def _dot_tn(a, b):
    return lax.dot_general(a, b, (((0,), (0,)), ((), ())), preferred_element_type=F32)


def _layernorm(z, g, b):
    mu = jnp.mean(z, axis=-1, keepdims=True)
    zc = z - mu
    var = jnp.mean(zc * zc, axis=-1, keepdims=True)
    return zc * lax.rsqrt(var + LN_EPS) * g + b


def _const_spec(shape):
    zeros = (0,) * len(shape)
    return pl.BlockSpec(shape, lambda *_: zeros, pipeline_mode=pl.Buffered(1))


def _params(sem):
    return pltpu.CompilerParams(dimension_semantics=sem, vmem_limit_bytes=VMEM_LIMIT)


def _retention_kernel(x_ref, cos_ref, sin_ref, dmask_ref, qdec_ref, kdec_ref, s0_ref,
                      win_ref, wout_ref, lrg_ref, lrb_ref, lmg_ref, lmb_ref,
                      y_ref, s_ref, *, state_decay):
    @pl.when(pl.program_id(1) == 0)
    def _():
        s_ref[...] = s0_ref[...]

    x = x_ref[0]
    xb = x.astype(BF16)
    cos = cos_ref[...]
    sin = sin_ref[...]
    half = RET_DK // 2

    def rope(u):
        u1, u2 = u[:, :half], u[:, half:]
        return jnp.concatenate([u1 * cos - u2 * sin, u2 * cos + u1 * sin], axis=1)

    mix = None
    for h in range(RET_HEADS):
        q = rope(_dot(xb, win_ref[:, h * RET_DK:(h + 1) * RET_DK]))
        k = rope(_dot(xb, win_ref[:, HK + h * RET_DK:HK + (h + 1) * RET_DK])) * (RET_DK ** -0.5)
        v = _dot(xb, win_ref[:, 2 * HK + h * RET_DV:2 * HK + (h + 1) * RET_DV])
        g = _dot(xb, win_ref[:, 2 * HK + HV + h * RET_DV:2 * HK + HV + (h + 1) * RET_DV])
        vb = v.astype(BF16)
        scores = _dot_nt(q.astype(BF16), k.astype(BF16)) * dmask_ref[h]
        state = s_ref[0, h]
        o = _dot(scores.astype(BF16), vb) + _dot((q * qdec_ref[h]).astype(BF16), state.astype(BF16))
        s_ref[0, h] = state_decay[h] * state + _dot_tn((k * kdec_ref[h]).astype(BF16), vb)
        sl = slice(h * RET_DV, (h + 1) * RET_DV)
        o = _layernorm(o, lrg_ref[:, sl], lrb_ref[:, sl])
        part = _dot((jax.nn.silu(g) * o).astype(BF16), wout_ref[sl, :])
        mix = part if mix is None else mix + part
    y_ref[0] = _layernorm(ALPHA * x + mix, lmg_ref[...], lmb_ref[...])


def _retention_layer(x, pos0, state0, w_in, w_out, ln_ret_g, ln_ret_b, ln_mix_g, ln_mix_b, tile):
    B, L, D = x.shape
    T = min(tile, L)
    assert L % T == 0
    nt = L // T
    pos = (pos0 + jnp.arange(L, dtype=jnp.int32)).astype(F32)
    half = RET_DK // 2
    inv = 1.0 / (ROPE_BASE ** (jnp.arange(half, dtype=F32) / half))
    ang = pos[:, None] * inv[None, :]
    cos, sin = jnp.cos(ang), jnp.sin(ang)
    lg = np.log1p(-np.exp2(-5.0 - np.arange(RET_HEADS, dtype=np.float64)))
    idx = np.arange(T, dtype=np.float64)
    diff = idx[:, None] - idx[None, :]
    dmask = np.where(diff >= 0, np.exp(lg[:, None, None] * np.maximum(diff, 0.0)), 0.0)
    qdec = np.exp((idx[None, :] + 1.0) * lg[:, None])[..., None]
    kdec = np.exp((T - 1.0 - idx)[None, :] * lg[:, None])[..., None]
    state_decay = tuple(float(v) for v in np.exp(T * lg))

    vec = lambda a: a.reshape(1, -1).astype(F32)
    kern = functools.partial(_retention_kernel, state_decay=state_decay)
    y, s_new = pl.pallas_call(
        kern,
        grid=(B, nt),
        in_specs=[
            pl.BlockSpec((1, T, D), lambda b, t: (b, t, 0)),
            pl.BlockSpec((T, half), lambda b, t: (t, 0)),
            pl.BlockSpec((T, half), lambda b, t: (t, 0)),
            _const_spec((RET_HEADS, T, T)),
            _const_spec((RET_HEADS, T, 1)),
            _const_spec((RET_HEADS, T, 1)),
            pl.BlockSpec((1, RET_HEADS, RET_DK, RET_DV), lambda b, t: (b, 0, 0, 0)),
            _const_spec(w_in.shape),
            _const_spec(w_out.shape),
            _const_spec((1, HV)), _const_spec((1, HV)),
            _const_spec((1, D)), _const_spec((1, D)),
        ],
        out_specs=[
            pl.BlockSpec((1, T, D), lambda b, t: (b, t, 0)),
            pl.BlockSpec((1, RET_HEADS, RET_DK, RET_DV), lambda b, t: (b, 0, 0, 0)),
        ],
        out_shape=[
            jax.ShapeDtypeStruct((B, L, D), F32),
            jax.ShapeDtypeStruct((B, RET_HEADS, RET_DK, RET_DV), F32),
        ],
        compiler_params=_params(("arbitrary", "arbitrary")),
        name="retention_layer",
    )(x, cos, sin, jnp.asarray(dmask, F32), jnp.asarray(qdec, F32), jnp.asarray(kdec, F32),
      state0, w_in, w_out, vec(ln_ret_g), vec(ln_ret_b), vec(ln_mix_g), vec(ln_mix_b))
    return y, s_new


def _ffn_kernel(*refs, n_valid, with_mix):
    if with_mix:
        (x_ref, ot_ref, wo_ref, lmg_ref, lmb_ref, cs0_ref, wup_ref, cw_ref, cb_ref, wdn_ref,
         lfg_ref, lfb_ref, y_ref, cs_ref, hid_ref) = refs
    else:
        (x_ref, cs0_ref, wup_ref, cw_ref, cb_ref, wdn_ref,
         lfg_ref, lfb_ref, y_ref, cs_ref, hid_ref) = refs

    @pl.when(pl.program_id(1) == 0)
    def _():
        cs_ref[...] = cs0_ref[...]

    x = x_ref[0]
    if with_mix:
        x = _layernorm(ALPHA * x + _dot_tn(ot_ref[0], wo_ref[...]), lmg_ref[...], lmb_ref[...])
    xb = x.astype(BF16)
    T = x.shape[0]
    row = lax.broadcasted_iota(jnp.int32, (T, FF_CHUNK), 0)
    for c in range(D_FF // FF_CHUNK):
        sl = slice(c * FF_CHUNK, (c + 1) * FF_CHUNK)
        val = _dot(xb, wup_ref[:, sl])
        a = _dot(xb, wup_ref[:, D_FF + c * FF_CHUNK:D_FF + (c + 1) * FF_CHUNK])
        prev2 = cs_ref[0, 0:1, sl]
        prev1 = cs_ref[0, 1:2, sl]
        a1 = jnp.where(row == 0, prev1, pltpu.roll(a, 1, 0))
        a2 = jnp.where(row == 0, prev2, jnp.where(row == 1, prev1, pltpu.roll(a, 2, 0)))
        conv = cb_ref[:, sl] + cw_ref[0:1, sl] * a2 + cw_ref[1:2, sl] * a1 + cw_ref[2:3, sl] * a
        gelu = 0.5 * conv * (1.0 + lax.erf(conv * (2.0 ** -0.5)))
        hid_ref[:, sl] = (gelu * val).astype(BF16)
        cs_ref[0, :, sl] = a[n_valid - 2:n_valid, :]
    f = _dot(hid_ref[...], wdn_ref[...])
    y_ref[0] = _layernorm(ALPHA * x + f, lfg_ref[...], lfb_ref[...])


def _ffn_layer(x, conv_state0, w_up, conv_w, conv_b, w_down, ln_g, ln_b, tile, n_valid=None,
               mix=None):
    B, L, D = x.shape
    T = min(tile, L)
    assert L % T == 0
    nt = L // T
    if n_valid is None:
        n_valid = T
    else:
        assert nt == 1
    vec = lambda a: a.reshape(1, -1).astype(F32)
    row_spec = pl.BlockSpec((1, T, D), lambda b, t: (b, t, 0))
    cs_spec = pl.BlockSpec((1, CONV_W - 1, D_FF), lambda b, t: (b, 0, 0))
    args, specs = [x], [row_spec]
    if mix is not None:
        o_t, w_o, lmg, lmb = mix
        args += [o_t, w_o, vec(lmg), vec(lmb)]
        specs += [pl.BlockSpec((1, HB, T), lambda b, t: (b, 0, t)), _const_spec(w_o.shape),
                  _const_spec((1, D)), _const_spec((1, D))]
    args += [conv_state0, w_up, conv_w, vec(conv_b), w_down, vec(ln_g), vec(ln_b)]
    specs += [cs_spec, _const_spec(w_up.shape), _const_spec(conv_w.shape), _const_spec((1, D_FF)),
              _const_spec(w_down.shape), _const_spec((1, D)), _const_spec((1, D))]
    kern = functools.partial(_ffn_kernel, n_valid=n_valid, with_mix=mix is not None)
    y, cs = pl.pallas_call(
        kern,
        grid=(B, nt),
        in_specs=specs,
        out_specs=[row_spec, cs_spec],
        out_shape=[jax.ShapeDtypeStruct((B, L, D), F32),
                   jax.ShapeDtypeStruct((B, CONV_W - 1, D_FF), F32)],
        scratch_shapes=[pltpu.VMEM((T, D_FF), BF16)],
        compiler_params=_params(("arbitrary", "arbitrary")),
        name="ffn_mix_layer" if mix is not None else "ffn_layer",
    )(*args)
    return y, cs


def _cumsum_lanes(x):
    n = x.shape[1]
    lane = lax.broadcasted_iota(jnp.int32, x.shape, 1)
    s = 1
    while s < n:
        x = x + jnp.where(lane >= s, pltpu.roll(x, s, 1), 0.0)
        s *= 2
    return x


def _split3(c):
    hi = c.astype(BF16).astype(F32)
    r = c - hi
    mid = r.astype(BF16).astype(F32)
    lo = (r - mid).astype(BF16).astype(F32)
    return hi, mid, lo


def _advance_cumsum(logf_t, carry_ref):
    c_t = carry_ref[:, 0:1] + _cumsum_lanes(logf_t)
    T = logf_t.shape[1]
    carry_ref[...] = jnp.broadcast_to(c_t[:, T - 1:T], carry_ref.shape)
    return c_t


def _store_k_operand(k, c_parts, place_ref, kp_ref):
    hi, mid, lo = c_parts
    T = k.shape[0]
    cp_t = jnp.concatenate(
        [-hi, -mid, -lo, jnp.ones((8, T), F32), jnp.zeros((LANES - 3 * FOX_HEADS - 8, T), F32)], axis=0)
    cp = cp_t.T.astype(BF16)
    aug = _dot(cp, place_ref[...])
    lane = lax.broadcasted_iota(jnp.int32, (T, AUG), 1)
    for p in range(FOX_HEADS // 2):
        slab = k[:, p * AUG:(p + 1) * AUG]
        h0, h1 = 2 * p, 2 * p + 1
        kp_ref[0, h0] = jnp.where(lane < FOX_DH, slab, aug[:, h0 * AUG:(h0 + 1) * AUG]).astype(BF16)
        kp_ref[0, h1] = jnp.where(lane < FOX_DH, pltpu.roll(slab, FOX_DH, 1),
                                  aug[:, h1 * AUG:(h1 + 1) * AUG]).astype(BF16)


def _store_v_operand(v_t, vp_ref):
    T = v_t.shape[1]
    sub = lax.broadcasted_iota(jnp.int32, (FOX_DH, T), 0)
    ones_row = jnp.where(sub == 0, 1.0, 0.0).astype(BF16)
    for h in range(FOX_HEADS):
        vp_ref[0, h, 0, 0:FOX_DH, :] = v_t[h * FOX_DH:(h + 1) * FOX_DH].astype(BF16)
        vp_ref[0, h, 0, FOX_DH:AUG, :] = ones_row


def _store_q_operand(q_t, c_parts, qp_ref):
    hi, mid, lo = c_parts
    T = q_t.shape[1]
    sub = lax.broadcasted_iota(jnp.int32, (FOX_DH, T), 0)
    for h in range(FOX_HEADS):
        aug = jnp.where(sub == 0, hi[h:h + 1],
              jnp.where(sub == 1, mid[h:h + 1],
              jnp.where(sub == 2, lo[h:h + 1],
              jnp.where(sub < 6, 1.0, 0.0))))
        qp_ref[0, h, 0, 0:FOX_DH, :] = q_t[h * FOX_DH:(h + 1) * FOX_DH].astype(BF16)
        qp_ref[0, h, 0, FOX_DH:AUG, :] = aug.astype(BF16)


def _placement_matrix():
    p = np.zeros((LANES, FOX_HEADS * AUG), np.float32)
    for h in range(FOX_HEADS):
        for part in range(3):
            p[part * FOX_HEADS + h, h * AUG + _COL_KC[part]] = 1.0
            p[3 * FOX_HEADS, h * AUG + _COL_QC[part]] = 1.0
    return jnp.asarray(p, BF16)


def _fox_proj_kernel(x_ref, carry0_ref, wk_ref, wv_ref, wvt_ref, wqt_ref, wft_ref, bf_ref, place_ref,
                     k_ref, v_ref, lf_ref, kp_ref, vp_ref, qp_ref, carry_ref):
    @pl.when(pl.program_id(1) == 0)
    def _():
        carry_ref[...] = carry0_ref[0]

    xb = x_ref[0].astype(BF16)
    k = _dot(xb, wk_ref[...])
    k_ref[0] = k
    v_ref[0] = _dot(xb, wv_ref[...])
    logf_t = jax.nn.log_sigmoid(_dot_nt(wft_ref[...], xb) + bf_ref[...])
    lf_ref[0] = logf_t
    c_parts = _split3(_advance_cumsum(logf_t, carry_ref))
    _store_k_operand(k, c_parts, place_ref, kp_ref)
    _store_v_operand(_dot_nt(wvt_ref[...], xb), vp_ref)
    _store_q_operand(_dot_nt(wqt_ref[...], xb), c_parts, qp_ref)


def _fox_projection(x, carry0, w_k, w_v, w_vt, w_qt, w_ft, b_f, tile):
    B, L, D = x.shape
    T = min(tile, L)
    assert L % T == 0 and T % LANES == 0
    nt = L // T
    H = FOX_HEADS
    return pl.pallas_call(
        _fox_proj_kernel,
        grid=(B, nt),
        in_specs=[
            pl.BlockSpec((1, T, D), lambda b, t: (b, t, 0)),
            pl.BlockSpec((1, H, LANES), lambda b, t: (b, 0, 0)),
            _const_spec(w_k.shape), _const_spec(w_v.shape), _const_spec(w_vt.shape),
            _const_spec(w_qt.shape), _const_spec(w_ft.shape), _const_spec((H, 1)),
            _const_spec((LANES, H * AUG)),
        ],
        out_specs=[
            pl.BlockSpec((1, T, HB), lambda b, t: (b, t, 0)),
            pl.BlockSpec((1, T, HB), lambda b, t: (b, t, 0)),
            pl.BlockSpec((1, H, T), lambda b, t: (b, 0, t)),
            pl.BlockSpec((1, H, T, AUG), lambda b, t: (b, 0, t, 0)),
            pl.BlockSpec((1, H, 1, AUG, T), lambda b, t: (b, 0, t, 0, 0)),
            pl.BlockSpec((1, H, 1, AUG, T), lambda b, t: (b, 0, t, 0, 0)),
        ],
        out_shape=[
            jax.ShapeDtypeStruct((B, L, HB), F32),
            jax.ShapeDtypeStruct((B, L, HB), F32),
            jax.ShapeDtypeStruct((B, H, L), F32),
            jax.ShapeDtypeStruct((B, H, L, AUG), BF16),
            jax.ShapeDtypeStruct((B, H, nt, AUG, T), BF16),
            jax.ShapeDtypeStruct((B, H, nt, AUG, T), BF16),
        ],
        scratch_shapes=[pltpu.VMEM((H, LANES), F32)],
        compiler_params=_params(("arbitrary", "arbitrary")),
        name="fox_projection",
    )(x, carry0, w_k, w_v, w_vt, w_qt, w_ft, b_f.reshape(H, 1).astype(F32), _placement_matrix())


def _cache_kernel(k_ref, v_ref, lf_ref, eye_ref, place_ref, kp_ref, vp_ref, carry_ref):
    @pl.when(pl.program_id(1) == 0)
    def _():
        carry_ref[...] = jnp.zeros_like(carry_ref)

    c_parts = _split3(_advance_cumsum(lf_ref[0], carry_ref.at[0]))
    _store_k_operand(k_ref[0], c_parts, place_ref, kp_ref)
    _store_v_operand(_dot_nt(eye_ref[...], v_ref[0].astype(BF16)), vp_ref)


def _cache_operands(cache_k, cache_v, cache_logf_t, tile):
    B, P, _ = cache_k.shape
    T = min(tile, P)
    assert P % T == 0 and T % LANES == 0
    nt = P // T
    H = FOX_HEADS
    return pl.pallas_call(
        _cache_kernel,
        grid=(B, nt),
        in_specs=[
            pl.BlockSpec((1, T, HB), lambda b, t: (b, t, 0)),
            pl.BlockSpec((1, T, HB), lambda b, t: (b, t, 0)),
            pl.BlockSpec((1, H, T), lambda b, t: (b, 0, t)),
            _const_spec((HB, HB)),
            _const_spec((LANES, H * AUG)),
        ],
        out_specs=[
            pl.BlockSpec((1, H, T, AUG), lambda b, t: (b, 0, t, 0)),
            pl.BlockSpec((1, H, 1, AUG, T), lambda b, t: (b, 0, t, 0, 0)),
            pl.BlockSpec((1, H, LANES), lambda b, t: (b, 0, 0)),
        ],
        out_shape=[
            jax.ShapeDtypeStruct((B, H, P, AUG), BF16),
            jax.ShapeDtypeStruct((B, H, nt, AUG, T), BF16),
            jax.ShapeDtypeStruct((B, H, LANES), F32),
        ],
        compiler_params=_params(("arbitrary", "arbitrary")),
        name="cache_operands",
    )(cache_k, cache_v, cache_logf_t, jnp.eye(HB, dtype=BF16), _placement_matrix())


def _finish_attention(acc):
    return (acc[0:FOX_DH] / acc[FOX_DH:FOX_DH + 1]).astype(BF16)


def _prompt_attn_kernel(q_ref, k_ref, v_ref, o_ref, m_ref, acc_ref):
    i = pl.program_id(2)
    q_t = q_ref[0, 0, 0]
    T = q_t.shape[1]

    k_blk = k_ref[0, 0, pl.ds(pl.multiple_of(i * T, T), T), :]
    s = _dot(k_blk, q_t)
    key = lax.broadcasted_iota(jnp.int32, (T, T), 0)
    qry = lax.broadcasted_iota(jnp.int32, (T, T), 1)
    s = jnp.where(key <= qry, s, NEG_INF)
    m = jnp.max(s, axis=0, keepdims=True)
    m_ref[...] = m
    acc_ref[...] = _dot(v_ref[0, 0, i], jnp.exp(s - m).astype(BF16))

    def body(j, carry):
        k_blk = k_ref[0, 0, pl.ds(pl.multiple_of(j * T, T), T), :]
        s = _dot(k_blk, q_t)
        m_prev = m_ref[...]
        m_new = jnp.maximum(m_prev, jnp.max(s, axis=0, keepdims=True))
        p = jnp.exp(s - m_new).astype(BF16)
        acc_ref[...] = jnp.exp(m_prev - m_new) * acc_ref[...] + _dot(v_ref[0, 0, j], p)
        m_ref[...] = m_new
        return carry

    lax.fori_loop(0, i, body, 0)
    o_ref[0] = _finish_attention(acc_ref[...])


def _prompt_attention(qp, kp, vp):
    B, H, nt, _, T = qp.shape
    L = nt * T
    return pl.pallas_call(
        _prompt_attn_kernel,
        grid=(B, H, nt),
        in_specs=[
            pl.BlockSpec((1, 1, 1, AUG, T), lambda b, h, i: (b, h, i, 0, 0)),
            pl.BlockSpec((1, 1, L, AUG), lambda b, h, i: (b, h, 0, 0)),
            pl.BlockSpec((1, 1, nt, AUG, T), lambda b, h, i: (b, h, 0, 0, 0)),
        ],
        out_specs=pl.BlockSpec((1, FOX_DH, T), lambda b, h, i: (b, h, i)),
        out_shape=jax.ShapeDtypeStruct((B, HB, L), BF16),
        scratch_shapes=[pltpu.VMEM((1, T), F32), pltpu.VMEM((AUG, T), F32)],
        compiler_params=_params(("arbitrary", "arbitrary", "arbitrary")),
        name="prompt_attention",
    )(qp, kp, vp)


def _sample_attn_kernel(q_ref, kc_ref, vc_ref, kn_ref, vn_ref, o_ref):
    q_t = q_ref[0, 0, 0]
    T = q_t.shape[1]
    n_c, _, t_c = vc_ref.shape[2:]
    s_c = _dot(kc_ref[0, 0], q_t)
    s_n = _dot(kn_ref[0, 0], q_t)
    key = lax.broadcasted_iota(jnp.int32, (T, T), 0)
    qry = lax.broadcasted_iota(jnp.int32, (T, T), 1)
    s_n = jnp.where(key <= qry, s_n, NEG_INF)
    m = jnp.maximum(jnp.max(s_c, axis=0, keepdims=True), jnp.max(s_n, axis=0, keepdims=True))
    p_c = jnp.exp(s_c - m).astype(BF16)
    acc = _dot(vn_ref[0, 0, 0], jnp.exp(s_n - m).astype(BF16))
    for j in range(n_c):
        acc = acc + _dot(vc_ref[0, 0, j], p_c[j * t_c:(j + 1) * t_c])
    o_ref[0] = _finish_attention(acc)


def _sample_attention(qp, kp_cache, vp_cache, kp_new, vp_new):
    B, H, _, _, T = qp.shape
    P = kp_cache.shape[2]
    n_c, _, t_c = vp_cache.shape[2:]
    return pl.pallas_call(
        _sample_attn_kernel,
        grid=(B, H),
        in_specs=[
            pl.BlockSpec((1, 1, 1, AUG, T), lambda b, h: (b, h, 0, 0, 0)),
            pl.BlockSpec((1, 1, P, AUG), lambda b, h: (b, h, 0, 0)),
            pl.BlockSpec((1, 1, n_c, AUG, t_c), lambda b, h: (b, h, 0, 0, 0)),
            pl.BlockSpec((1, 1, T, AUG), lambda b, h: (b, h, 0, 0)),
            pl.BlockSpec((1, 1, 1, AUG, T), lambda b, h: (b, h, 0, 0, 0)),
        ],
        out_specs=pl.BlockSpec((1, FOX_DH, T), lambda b, h: (b, h, 0)),
        out_shape=jax.ShapeDtypeStruct((B, HB, T), BF16),
        compiler_params=_params(("arbitrary", "arbitrary")),
        name="sample_attention",
    )(qp, kp_cache, vp_cache, kp_new, vp_new)


RET_TILE = 256
FFN_TILE = 256
FOX_TILE = 512
CACHE_TILE = 512


def kernel(x_prompt, x_sample, cache_k, cache_v, cache_logf, state_ret, state_ffn_conv,
           w_in_a, ln_ret_g, ln_ret_b, w_out_a, w_kvf, b_f, w_q_b, w_out_b,
           ln_mix_g, ln_mix_b, w_up, conv_w, conv_b, w_down, ln_ffn_g, ln_ffn_b):
    Bp, Lp, _ = x_prompt.shape
    Bs, Ls, _ = x_sample.shape
    past = cache_k.shape[1]
    H = FOX_HEADS

    w_in = w_in_a[0].astype(BF16)
    w_oa = w_out_a[0].astype(BF16)
    w_k = w_kvf[:, :HB].astype(BF16)
    w_v = w_kvf[:, HB:2 * HB].astype(BF16)
    w_vt = w_v.T
    w_ft = w_kvf[:, 2 * HB:].T.astype(BF16)
    w_qt = (w_q_b[0] * (FOX_DH ** -0.5)).T.astype(BF16)
    w_ob = w_out_b[0].astype(BF16)
    w_up_b = w_up.astype(BF16)
    w_dn_b = w_down.astype(BF16)

    def layer0(x, pos0, ret_state0, conv_state0, ret_tile, ffn_tile):
        x1, ret_new = _retention_layer(x, pos0, ret_state0, w_in, w_oa, ln_ret_g[0], ln_ret_b[0],
                                       ln_mix_g[0], ln_mix_b[0], ret_tile)
        x2, cs0 = _ffn_layer(x1, conv_state0, w_up_b[0], conv_w[0], conv_b[0], w_dn_b[0],
                             ln_ffn_g[0], ln_ffn_b[0], ffn_tile)
        return x2, ret_new, cs0

    def ffn1(x2, o_t, conv_state0, tile, n_valid=None):
        return _ffn_layer(x2, conv_state0, w_up_b[1], conv_w[1], conv_b[1], w_dn_b[1],
                          ln_ffn_g[1], ln_ffn_b[1], tile, n_valid=n_valid,
                          mix=(o_t, w_ob, ln_mix_g[1], ln_mix_b[1]))

    zeros_ret = jnp.zeros((Bp, RET_HEADS, RET_DK, RET_DV), F32)
    zeros_cs = jnp.zeros((Bp, CONV_W - 1, D_FF), F32)
    x2p, ret_p, cs0_p = layer0(x_prompt, 0, zeros_ret, zeros_cs, RET_TILE, FFN_TILE)
    kp_, vp_, lft_p, kop, vop, qop = _fox_projection(
        x2p, jnp.zeros((Bp, H, LANES), F32), w_k, w_v, w_vt, w_qt, w_ft, b_f, FOX_TILE)
    ot_p = _prompt_attention(qop, kop, vop)
    y_p, cs1_p = ffn1(x2p, ot_p, zeros_cs, FFN_TILE)

    x2s, ret_s, cs0_s = layer0(x_sample, past, state_ret[0], state_ffn_conv[0], Ls, Ls)
    kc, vc, carry = _cache_operands(cache_k.reshape(Bs, past, HB), cache_v.reshape(Bs, past, HB),
                                    jnp.transpose(cache_logf.astype(F32), (0, 2, 1)), CACHE_TILE)
    Tpad = max(LANES, Ls)
    x2s_pad = jnp.pad(x2s, ((0, 0), (0, Tpad - Ls), (0, 0)))
    ks_, vs_, lft_s, kos, vos, qos = _fox_projection(x2s_pad, carry, w_k, w_v, w_vt, w_qt, w_ft, b_f, Tpad)
    ot_s = _sample_attention(qos, kc, vc, kos, vos)
    y_s_pad, cs1_s = ffn1(x2s_pad, ot_s, state_ffn_conv[1], Tpad, n_valid=Ls)

    def heads(a, L):
        return a[:, :L].reshape(a.shape[0], L, H, FOX_DH)

    return (y_p, y_s_pad[:, :Ls],
            ret_p[None], heads(kp_, Lp), heads(vp_, Lp), jnp.transpose(lft_p, (0, 2, 1)),
            jnp.stack([cs0_p, cs1_p]),
            ret_s[None], heads(ks_, Ls), heads(vs_, Ls), jnp.transpose(lft_s[:, :, :Ls], (0, 2, 1)),
            jnp.stack([cs0_s, cs1_s]))
```

```python
import functools
import math

import numpy as np
import jax
import jax.numpy as jnp
from jax import lax
from jax.experimental import pallas as pl
from jax.experimental.pallas import tpu as pltpu

F32 = jnp.float32
BF16 = jnp.bfloat16

D_MODEL = 1024
RET_DK = 256
RET_HEADS = D_MODEL // RET_DK
RET_DV = 2 * RET_DK
HK = RET_HEADS * RET_DK
HV = RET_HEADS * RET_DV
FOX_DH = 64
FOX_HEADS = D_MODEL // FOX_DH
HB = FOX_HEADS * FOX_DH
D_FF = 2816
CONV_W = 3
ROPE_BASE = 10000.0
LN_EPS = 1e-5
DEPTH = 2
ALPHA = (2 * DEPTH) ** 0.25
NEG_INF = -1e30
LOG2E = math.log2(math.e)

LANES = 128
AUG = 2 * FOX_DH
FF_CHUNK = 256
VMEM_LIMIT = 56 * 1024 * 1024

_COL_QC = (FOX_DH, FOX_DH + 1, FOX_DH + 2)
_COL_KC = (FOX_DH + 3, FOX_DH + 4, FOX_DH + 5)


def _dot(a, b):
    return jnp.dot(a, b, preferred_element_type=F32)


def _dot_nt(a, b):
    return lax.dot_general(a, b, (((1,), (1,)), ((), ())), preferred_element_type=F32)


def _dot_tn(a, b):
    return lax.dot_general(a, b, (((0,), (0,)), ((), ())), preferred_element_type=F32)


def _layernorm(z, g, b):
    mu = jnp.mean(z, axis=-1, keepdims=True)
    zc = z - mu
    var = jnp.mean(zc * zc, axis=-1, keepdims=True)
    return zc * lax.rsqrt(var + LN_EPS) * g + b


def _const_spec(shape):
    zeros = (0,) * len(shape)
    return pl.BlockSpec(shape, lambda *_: zeros, pipeline_mode=pl.Buffered(1))


def _params(sem, flags=None):
    return pltpu.CompilerParams(dimension_semantics=sem, vmem_limit_bytes=VMEM_LIMIT, flags=flags)


def _retention_kernel(x_ref, cos_ref, sin_ref, dmask_ref, qdec_ref, kdec_ref, s0_ref,
                      win_ref, wout_ref, lrg_ref, lrb_ref, lmg_ref, lmb_ref,
                      y_ref, s_ref, *, state_decay):
    @pl.when(pl.program_id(1) == 0)
    def _():
        s_ref[...] = s0_ref[...]

    x = x_ref[0]
    xb = x.astype(BF16)
    cos = cos_ref[...]
    sin = sin_ref[...]
    half = RET_DK // 2

    def rope(u):
        u1, u2 = u[:, :half], u[:, half:]
        return jnp.concatenate([u1 * cos - u2 * sin, u2 * cos + u1 * sin], axis=1)

    mix = None
    for h in range(RET_HEADS):
        q = rope(_dot(xb, win_ref[:, h * RET_DK:(h + 1) * RET_DK]))
        k = rope(_dot(xb, win_ref[:, HK + h * RET_DK:HK + (h + 1) * RET_DK])) * (RET_DK ** -0.5)
        v = _dot(xb, win_ref[:, 2 * HK + h * RET_DV:2 * HK + (h + 1) * RET_DV])
        g = _dot(xb, win_ref[:, 2 * HK + HV + h * RET_DV:2 * HK + HV + (h + 1) * RET_DV])
        vb = v.astype(BF16)
        scores = _dot_nt(q.astype(BF16), k.astype(BF16)) * dmask_ref[h]
        state = s_ref[0, h]
        o = _dot(scores.astype(BF16), vb) + _dot((q * qdec_ref[h]).astype(BF16), state.astype(BF16))
        s_ref[0, h] = state_decay[h] * state + _dot_tn((k * kdec_ref[h]).astype(BF16), vb)
        sl = slice(h * RET_DV, (h + 1) * RET_DV)
        o = _layernorm(o, lrg_ref[:, sl], lrb_ref[:, sl])
        part = _dot((jax.nn.silu(g) * o).astype(BF16), wout_ref[sl, :])
        mix = part if mix is None else mix + part
    y_ref[0] = _layernorm(ALPHA * x + mix, lmg_ref[...], lmb_ref[...])


def _retention_layer(x, pos0, state0, w_in, w_out, ln_ret_g, ln_ret_b, ln_mix_g, ln_mix_b, tile):
    B, L, D = x.shape
    T = min(tile, L)
    assert L % T == 0
    nt = L // T
    pos = (pos0 + jnp.arange(L, dtype=jnp.int32)).astype(F32)
    half = RET_DK // 2
    inv = 1.0 / (ROPE_BASE ** (jnp.arange(half, dtype=F32) / half))
    ang = pos[:, None] * inv[None, :]
    cos, sin = jnp.cos(ang), jnp.sin(ang)
    lg = np.log1p(-np.exp2(-5.0 - np.arange(RET_HEADS, dtype=np.float64)))
    idx = np.arange(T, dtype=np.float64)
    diff = idx[:, None] - idx[None, :]
    dmask = np.where(diff >= 0, np.exp(lg[:, None, None] * np.maximum(diff, 0.0)), 0.0)
    qdec = np.exp((idx[None, :] + 1.0) * lg[:, None])[..., None]
    kdec = np.exp((T - 1.0 - idx)[None, :] * lg[:, None])[..., None]
    state_decay = tuple(float(v) for v in np.exp(T * lg))

    vec = lambda a: a.reshape(1, -1).astype(F32)
    kern = functools.partial(_retention_kernel, state_decay=state_decay)
    y, s_new = pl.pallas_call(
        kern,
        grid=(B, nt),
        in_specs=[
            pl.BlockSpec((1, T, D), lambda b, t: (b, t, 0)),
            pl.BlockSpec((T, half), lambda b, t: (t, 0)),
            pl.BlockSpec((T, half), lambda b, t: (t, 0)),
            _const_spec((RET_HEADS, T, T)),
            _const_spec((RET_HEADS, T, 1)),
            _const_spec((RET_HEADS, T, 1)),
            pl.BlockSpec((1, RET_HEADS, RET_DK, RET_DV), lambda b, t: (b, 0, 0, 0)),
            _const_spec(w_in.shape),
            _const_spec(w_out.shape),
            _const_spec((1, HV)), _const_spec((1, HV)),
            _const_spec((1, D)), _const_spec((1, D)),
        ],
        out_specs=[
            pl.BlockSpec((1, T, D), lambda b, t: (b, t, 0)),
            pl.BlockSpec((1, RET_HEADS, RET_DK, RET_DV), lambda b, t: (b, 0, 0, 0)),
        ],
        out_shape=[
            jax.ShapeDtypeStruct((B, L, D), F32),
            jax.ShapeDtypeStruct((B, RET_HEADS, RET_DK, RET_DV), F32),
        ],
        compiler_params=_params(("arbitrary", "arbitrary")),
        name="retention_layer",
    )(x, cos, sin, jnp.asarray(dmask, F32), jnp.asarray(qdec, F32), jnp.asarray(kdec, F32),
      state0, w_in, w_out, vec(ln_ret_g), vec(ln_ret_b), vec(ln_mix_g), vec(ln_mix_b))
    return y, s_new


def _ffn_kernel(*refs, n_valid, with_mix):
    if with_mix:
        (x_ref, ot_ref, wo_ref, lmg_ref, lmb_ref, cs0_ref, wup_ref, cw_ref, cb_ref, wdn_ref,
         lfg_ref, lfb_ref, y_ref, cs_ref, hid_ref) = refs
    else:
        (x_ref, cs0_ref, wup_ref, cw_ref, cb_ref, wdn_ref,
         lfg_ref, lfb_ref, y_ref, cs_ref, hid_ref) = refs

    @pl.when(pl.program_id(1) == 0)
    def _():
        cs_ref[...] = cs0_ref[...]

    x = x_ref[0]
    if with_mix:
        o_t = ot_ref[0, :, 0].reshape(HB, x.shape[0])
        x = _layernorm(ALPHA * x + _dot_tn(o_t, wo_ref[...]), lmg_ref[...], lmb_ref[...])
    xb = x.astype(BF16)
    T = x.shape[0]
    row = lax.broadcasted_iota(jnp.int32, (T, FF_CHUNK), 0)
    for c in range(D_FF // FF_CHUNK):
        sl = slice(c * FF_CHUNK, (c + 1) * FF_CHUNK)
        val = _dot(xb, wup_ref[:, sl])
        a = _dot(xb, wup_ref[:, D_FF + c * FF_CHUNK:D_FF + (c + 1) * FF_CHUNK])
        prev2 = cs_ref[0, 0:1, sl]
        prev1 = cs_ref[0, 1:2, sl]
        a1 = jnp.where(row == 0, prev1, pltpu.roll(a, 1, 0))
        a2 = jnp.where(row == 0, prev2, jnp.where(row == 1, prev1, pltpu.roll(a, 2, 0)))
        conv = cb_ref[:, sl] + cw_ref[0:1, sl] * a2 + cw_ref[1:2, sl] * a1 + cw_ref[2:3, sl] * a
        gelu = 0.5 * conv * (1.0 + lax.erf(conv * (2.0 ** -0.5)))
        hid_ref[:, sl] = (gelu * val).astype(BF16)
        cs_ref[0, :, sl] = a[n_valid - 2:n_valid, :]
    f = _dot(hid_ref[...], wdn_ref[...])
    y_ref[0] = _layernorm(ALPHA * x + f, lfg_ref[...], lfb_ref[...])


def _ffn_layer(x, conv_state0, w_up, conv_w, conv_b, w_down, ln_g, ln_b, tile, n_valid=None,
               mix=None):
    B, L, D = x.shape
    T = min(tile, L)
    assert L % T == 0
    nt = L // T
    if n_valid is None:
        n_valid = T
    else:
        assert nt == 1
    vec = lambda a: a.reshape(1, -1).astype(F32)
    row_spec = pl.BlockSpec((1, T, D), lambda b, t: (b, t, 0))
    cs_spec = pl.BlockSpec((1, CONV_W - 1, D_FF), lambda b, t: (b, 0, 0))
    args, specs = [x], [row_spec]
    if mix is not None:
        o_t, w_o, lmg, lmb = mix
        args += [o_t, w_o, vec(lmg), vec(lmb)]
        per_blk = o_t.shape[4] // T
        assert o_t.shape[4] % T == 0 and o_t.shape[2] * per_blk == nt
        specs += [pl.BlockSpec((1, FOX_HEADS, 1, FOX_DH, T),
                               lambda b, t: (b, 0, t // per_blk, 0, t % per_blk)),
                  _const_spec(w_o.shape),
                  _const_spec((1, D)), _const_spec((1, D))]
    args += [conv_state0, w_up, conv_w, vec(conv_b), w_down, vec(ln_g), vec(ln_b)]
    specs += [cs_spec, _const_spec(w_up.shape), _const_spec(conv_w.shape), _const_spec((1, D_FF)),
              _const_spec(w_down.shape), _const_spec((1, D)), _const_spec((1, D))]
    kern = functools.partial(_ffn_kernel, n_valid=n_valid, with_mix=mix is not None)
    y, cs = pl.pallas_call(
        kern,
        grid=(B, nt),
        in_specs=specs,
        out_specs=[row_spec, cs_spec],
        out_shape=[jax.ShapeDtypeStruct((B, L, D), F32),
                   jax.ShapeDtypeStruct((B, CONV_W - 1, D_FF), F32)],
        scratch_shapes=[pltpu.VMEM((T, D_FF), BF16)],
        compiler_params=_params(("arbitrary", "arbitrary")),
        name="ffn_mix_layer" if mix is not None else "ffn_layer",
    )(*args)
    return y, cs


def _cumsum_lanes(x):
    n = x.shape[1]
    lane = lax.broadcasted_iota(jnp.int32, x.shape, 1)
    s = 1
    while s < n:
        x = x + jnp.where(lane >= s, pltpu.roll(x, s, 1), 0.0)
        s *= 2
    return x


def _split3(c):
    hi = c.astype(BF16).astype(F32)
    r = c - hi
    mid = r.astype(BF16).astype(F32)
    lo = (r - mid).astype(BF16).astype(F32)
    return hi, mid, lo


def _advance_cumsum(logf_t, carry_ref):
    c_t = carry_ref[:, 0:1] + _cumsum_lanes(logf_t)
    T = logf_t.shape[1]
    carry_ref[...] = jnp.broadcast_to(c_t[:, T - 1:T], carry_ref.shape)
    return c_t


def _store_k_operand(k, c_parts, place_ref, kp_ref):
    hi, mid, lo = c_parts
    T = k.shape[0]
    cp_t = jnp.concatenate(
        [-hi, -mid, -lo, jnp.ones((8, T), F32), jnp.zeros((LANES - 3 * FOX_HEADS - 8, T), F32)], axis=0)
    cp = cp_t.T.astype(BF16)
    aug = _dot(cp, place_ref[...])
    lane = lax.broadcasted_iota(jnp.int32, (T, AUG), 1)
    for p in range(FOX_HEADS // 2):
        slab = k[:, p * AUG:(p + 1) * AUG]
        h0, h1 = 2 * p, 2 * p + 1
        kp_ref[0, h0] = jnp.where(lane < FOX_DH, slab, aug[:, h0 * AUG:(h0 + 1) * AUG]).astype(BF16)
        kp_ref[0, h1] = jnp.where(lane < FOX_DH, pltpu.roll(slab, FOX_DH, 1),
                                  aug[:, h1 * AUG:(h1 + 1) * AUG]).astype(BF16)


def _store_v_operand(v_t, vp_ref):
    T = v_t.shape[1]
    sub = lax.broadcasted_iota(jnp.int32, (FOX_DH, T), 0)
    ones_row = jnp.where(sub == 0, 1.0, 0.0).astype(BF16)
    for h in range(FOX_HEADS):
        vp_ref[0, h, 0, 0:FOX_DH, :] = v_t[h * FOX_DH:(h + 1) * FOX_DH].astype(BF16)
        vp_ref[0, h, 0, FOX_DH:AUG, :] = ones_row


def _store_q_operand(q_t, c_parts, qp_ref):
    hi, mid, lo = c_parts
    T = q_t.shape[1]
    sub = lax.broadcasted_iota(jnp.int32, (FOX_DH, T), 0)
    for h in range(FOX_HEADS):
        aug = jnp.where(sub == 0, hi[h:h + 1],
              jnp.where(sub == 1, mid[h:h + 1],
              jnp.where(sub == 2, lo[h:h + 1],
              jnp.where(sub < 6, 1.0, 0.0))))
        qp_ref[0, h, 0, 0:FOX_DH, :] = q_t[h * FOX_DH:(h + 1) * FOX_DH].astype(BF16)
        qp_ref[0, h, 0, FOX_DH:AUG, :] = aug.astype(BF16)


def _placement_matrix():
    p = np.zeros((LANES, FOX_HEADS * AUG), np.float32)
    for h in range(FOX_HEADS):
        for part in range(3):
            p[part * FOX_HEADS + h, h * AUG + _COL_KC[part]] = 1.0
            p[3 * FOX_HEADS, h * AUG + _COL_QC[part]] = 1.0
    return jnp.asarray(p, BF16)


def _fox_proj_kernel(x_ref, carry0_ref, wk_ref, wv_ref, wvt_ref, wqt_ref, wft_ref, bf_ref, place_ref,
                     k_ref, v_ref, lf_ref, kp_ref, vp_ref, qp_ref, carry_ref):
    @pl.when(pl.program_id(1) == 0)
    def _():
        carry_ref[...] = carry0_ref[0]

    xb = x_ref[0].astype(BF16)
    k = _dot(xb, wk_ref[...])
    k_ref[0] = k
    v_ref[0] = _dot(xb, wv_ref[...])
    logf_t = jax.nn.log_sigmoid(_dot_nt(wft_ref[...], xb) + bf_ref[...])
    lf_ref[0] = logf_t
    c_parts = _split3(_advance_cumsum(logf_t, carry_ref) * LOG2E)
    _store_k_operand(k, c_parts, place_ref, kp_ref)
    _store_v_operand(_dot_nt(wvt_ref[...], xb), vp_ref)
    _store_q_operand(_dot_nt(wqt_ref[...], xb), c_parts, qp_ref)


def _fox_projection(x, carry0, w_k, w_v, w_vt, w_qt, w_ft, b_f, tile):
    B, L, D = x.shape
    T = min(tile, L)
    assert L % T == 0 and T % LANES == 0
    nt = L // T
    H = FOX_HEADS
    return pl.pallas_call(
        _fox_proj_kernel,
        grid=(B, nt),
        in_specs=[
            pl.BlockSpec((1, T, D), lambda b, t: (b, t, 0)),
            pl.BlockSpec((1, H, LANES), lambda b, t: (b, 0, 0)),
            _const_spec(w_k.shape), _const_spec(w_v.shape), _const_spec(w_vt.shape),
            _const_spec(w_qt.shape), _const_spec(w_ft.shape), _const_spec((H, 1)),
            _const_spec((LANES, H * AUG)),
        ],
        out_specs=[
            pl.BlockSpec((1, T, HB), lambda b, t: (b, t, 0)),
            pl.BlockSpec((1, T, HB), lambda b, t: (b, t, 0)),
            pl.BlockSpec((1, H, T), lambda b, t: (b, 0, t)),
            pl.BlockSpec((1, H, T, AUG), lambda b, t: (b, 0, t, 0)),
            pl.BlockSpec((1, H, 1, AUG, T), lambda b, t: (b, 0, t, 0, 0)),
            pl.BlockSpec((1, H, 1, AUG, T), lambda b, t: (b, 0, t, 0, 0)),
        ],
        out_shape=[
            jax.ShapeDtypeStruct((B, L, HB), F32),
            jax.ShapeDtypeStruct((B, L, HB), F32),
            jax.ShapeDtypeStruct((B, H, L), F32),
            jax.ShapeDtypeStruct((B, H, L, AUG), BF16),
            jax.ShapeDtypeStruct((B, H, nt, AUG, T), BF16),
            jax.ShapeDtypeStruct((B, H, nt, AUG, T), BF16),
        ],
        scratch_shapes=[pltpu.VMEM((H, LANES), F32)],
        compiler_params=_params(("arbitrary", "arbitrary")),
        name="fox_projection",
    )(x, carry0, w_k, w_v, w_vt, w_qt, w_ft, b_f.reshape(H, 1).astype(F32), _placement_matrix())


def _cache_kernel(k_ref, v_ref, lf_ref, eye_ref, place_ref, kp_ref, vp_ref, carry_ref):
    @pl.when(pl.program_id(1) == 0)
    def _():
        carry_ref[...] = jnp.zeros_like(carry_ref)

    c_parts = _split3(_advance_cumsum(lf_ref[0], carry_ref.at[0]) * LOG2E)
    _store_k_operand(k_ref[0], c_parts, place_ref, kp_ref)
    _store_v_operand(_dot_nt(eye_ref[...], v_ref[0].astype(BF16)), vp_ref)


def _cache_operands(cache_k, cache_v, cache_logf_t, tile):
    B, P, _ = cache_k.shape
    T = min(tile, P)
    assert P % T == 0 and T % LANES == 0
    nt = P // T
    H = FOX_HEADS
    return pl.pallas_call(
        _cache_kernel,
        grid=(B, nt),
        in_specs=[
            pl.BlockSpec((1, T, HB), lambda b, t: (b, t, 0)),
            pl.BlockSpec((1, T, HB), lambda b, t: (b, t, 0)),
            pl.BlockSpec((1, H, T), lambda b, t: (b, 0, t)),
            _const_spec((HB, HB)),
            _const_spec((LANES, H * AUG)),
        ],
        out_specs=[
            pl.BlockSpec((1, H, T, AUG), lambda b, t: (b, 0, t, 0)),
            pl.BlockSpec((1, H, 1, AUG, T), lambda b, t: (b, 0, t, 0, 0)),
            pl.BlockSpec((1, H, LANES), lambda b, t: (b, 0, 0)),
        ],
        out_shape=[
            jax.ShapeDtypeStruct((B, H, P, AUG), BF16),
            jax.ShapeDtypeStruct((B, H, nt, AUG, T), BF16),
            jax.ShapeDtypeStruct((B, H, LANES), F32),
        ],
        compiler_params=_params(("arbitrary", "arbitrary")),
        name="cache_operands",
    )(cache_k, cache_v, cache_logf_t, jnp.eye(HB, dtype=BF16), _placement_matrix())


def _finish_attention(acc):
    return (acc[0:FOX_DH] / acc[FOX_DH:FOX_DH + 1]).astype(BF16)


ATTN_UNROLL = 8


def _prompt_attn_kernel(ti_ref, tj_ref, q_ref, k_ref, v_ref, o_ref,
                        sa_ref, sb_ref, bma_ref, bmb_ref, m_ref, acc_ref, *, n_trips):
    nq, _, T = q_ref.shape[2:]

    def k_block(j):
        return k_ref[0, 0, pl.ds(pl.multiple_of(j * T, T), T), :]

    m_ref[nq] = jnp.zeros((1, T), F32)
    acc_ref[nq] = jnp.zeros((AUG, T), F32)

    key = lax.broadcasted_iota(jnp.int32, (T, T), 0)
    qry = lax.broadcasted_iota(jnp.int32, (T, T), 1)

    def diagonal(i, carry):
        s = jnp.where(key <= qry, _dot(k_block(i), q_ref[0, 0, i]), NEG_INF)
        m = jnp.max(s, axis=0, keepdims=True)
        m_ref[i] = m
        acc_ref[i] = _dot(v_ref[0, 0, i], jnp.exp2(s - m).astype(BF16))
        return carry

    lax.fori_loop(0, nq, diagonal, 0)

    bufs = ((sa_ref, bma_ref), (sb_ref, bmb_ref))

    def produce(t, slot):
        s_ref, bm_ref = bufs[slot]
        s = _dot(k_block(tj_ref[t]), q_ref[0, 0, jnp.minimum(ti_ref[t], nq - 1)])
        s_ref[...] = s
        bm_ref[...] = jnp.max(s, axis=0, keepdims=True)

    def consume(t, slot):
        s_ref, bm_ref = bufs[slot]
        i = ti_ref[t]
        m_prev = m_ref[i]
        m_new = jnp.maximum(m_prev, bm_ref[...])
        p = jnp.exp2(s_ref[...] - m_new).astype(BF16)
        acc_ref[i] = jnp.exp2(m_prev - m_new) * acc_ref[i] + _dot(v_ref[0, 0, tj_ref[t]], p)
        m_ref[i] = m_new

    produce(0, 0)

    def trip(u, carry):
        t0 = ATTN_UNROLL * u
        for k in range(ATTN_UNROLL):
            produce(t0 + k + 1, (k + 1) % 2)
            consume(t0 + k, k % 2)
        return carry

    lax.fori_loop(0, n_trips, trip, 0)

    def finish(i, carry):
        o_ref[0, 0, i] = _finish_attention(acc_ref[i])
        return carry

    lax.fori_loop(0, nq, finish, 0)


def _prompt_attention(qp, kp, vp):
    B, H, nq, _, T = qp.shape
    L = nq * T
    tasks = [(i, j) for i in range(nq) for j in range(i)]
    n_trips = -(-len(tasks) // ATTN_UNROLL)
    tasks += [(nq, 0)] * (n_trips * ATTN_UNROLL + 1 - len(tasks))
    ti = jnp.asarray([t[0] for t in tasks], jnp.int32)
    tj = jnp.asarray([t[1] for t in tasks], jnp.int32)
    grid_spec = pltpu.PrefetchScalarGridSpec(
        num_scalar_prefetch=2,
        grid=(B, H),
        in_specs=[
            pl.BlockSpec((1, 1, nq, AUG, T), lambda b, h, ti, tj: (b, h, 0, 0, 0)),
            pl.BlockSpec((1, 1, L, AUG), lambda b, h, ti, tj: (b, h, 0, 0)),
            pl.BlockSpec((1, 1, nq, AUG, T), lambda b, h, ti, tj: (b, h, 0, 0, 0)),
        ],
        out_specs=pl.BlockSpec((1, 1, nq, FOX_DH, T), lambda b, h, ti, tj: (b, h, 0, 0, 0)),
        scratch_shapes=[
            pltpu.VMEM((T, T), F32), pltpu.VMEM((T, T), F32),
            pltpu.VMEM((1, T), F32), pltpu.VMEM((1, T), F32),
            pltpu.VMEM((nq + 1, 1, T), F32),
            pltpu.VMEM((nq + 1, AUG, T), F32),
        ],
    )
    return pl.pallas_call(
        functools.partial(_prompt_attn_kernel, n_trips=n_trips),
        grid_spec=grid_spec,
        out_shape=jax.ShapeDtypeStruct((B, H, nq, FOX_DH, T), BF16),
        compiler_params=_params(("arbitrary", "arbitrary")),
        name="prompt_attention",
    )(ti, tj, qp, kp, vp)


def _sample_attn_kernel(q_ref, kc_ref, vc_ref, kn_ref, vn_ref, o_ref):
    q_t = q_ref[0, 0, 0]
    T = q_t.shape[1]
    n_c, _, t_c = vc_ref.shape[2:]
    s_c = _dot(kc_ref[0, 0], q_t)
    s_n = _dot(kn_ref[0, 0], q_t)
    key = lax.broadcasted_iota(jnp.int32, (T, T), 0)
    qry = lax.broadcasted_iota(jnp.int32, (T, T), 1)
    s_n = jnp.where(key <= qry, s_n, NEG_INF)
    m = jnp.maximum(jnp.max(s_c, axis=0, keepdims=True), jnp.max(s_n, axis=0, keepdims=True))
    p_c = jnp.exp2(s_c - m).astype(BF16)
    acc = _dot(vn_ref[0, 0, 0], jnp.exp2(s_n - m).astype(BF16))
    for j in range(n_c):
        acc = acc + _dot(vc_ref[0, 0, j], p_c[j * t_c:(j + 1) * t_c])
    o_ref[0, 0, 0] = _finish_attention(acc)


def _sample_attention(qp, kp_cache, vp_cache, kp_new, vp_new):
    B, H, _, _, T = qp.shape
    P = kp_cache.shape[2]
    n_c, _, t_c = vp_cache.shape[2:]
    return pl.pallas_call(
        _sample_attn_kernel,
        grid=(B, H),
        in_specs=[
            pl.BlockSpec((1, 1, 1, AUG, T), lambda b, h: (b, h, 0, 0, 0)),
            pl.BlockSpec((1, 1, P, AUG), lambda b, h: (b, h, 0, 0)),
            pl.BlockSpec((1, 1, n_c, AUG, t_c), lambda b, h: (b, h, 0, 0, 0)),
            pl.BlockSpec((1, 1, T, AUG), lambda b, h: (b, h, 0, 0)),
            pl.BlockSpec((1, 1, 1, AUG, T), lambda b, h: (b, h, 0, 0, 0)),
        ],
        out_specs=pl.BlockSpec((1, 1, 1, FOX_DH, T), lambda b, h: (b, h, 0, 0, 0)),
        out_shape=jax.ShapeDtypeStruct((B, H, 1, FOX_DH, T), BF16),
        compiler_params=_params(("arbitrary", "arbitrary")),
        name="sample_attention",
    )(qp, kp_cache, vp_cache, kp_new, vp_new)


RET_TILE = 256
FFN_TILE = 256
FOX_TILE = 512
CACHE_TILE = 512


def kernel(x_prompt, x_sample, cache_k, cache_v, cache_logf, state_ret, state_ffn_conv,
           w_in_a, ln_ret_g, ln_ret_b, w_out_a, w_kvf, b_f, w_q_b, w_out_b,
           ln_mix_g, ln_mix_b, w_up, conv_w, conv_b, w_down, ln_ffn_g, ln_ffn_b):
    Bp, Lp, _ = x_prompt.shape
    Bs, Ls, _ = x_sample.shape
    past = cache_k.shape[1]
    H = FOX_HEADS

    w_in = w_in_a[0].astype(BF16)
    w_oa = w_out_a[0].astype(BF16)
    w_k = w_kvf[:, :HB].astype(BF16)
    w_v = w_kvf[:, HB:2 * HB].astype(BF16)
    w_vt = w_v.T
    w_ft = w_kvf[:, 2 * HB:].T.astype(BF16)
    w_qt = (w_q_b[0] * (FOX_DH ** -0.5 * LOG2E)).T.astype(BF16)
    w_ob = w_out_b[0].astype(BF16)
    w_up_b = w_up.astype(BF16)
    w_dn_b = w_down.astype(BF16)

    def layer0(x, pos0, ret_state0, conv_state0, ret_tile, ffn_tile):
        x1, ret_new = _retention_layer(x, pos0, ret_state0, w_in, w_oa, ln_ret_g[0], ln_ret_b[0],
                                       ln_mix_g[0], ln_mix_b[0], ret_tile)
        x2, cs0 = _ffn_layer(x1, conv_state0, w_up_b[0], conv_w[0], conv_b[0], w_dn_b[0],
                             ln_ffn_g[0], ln_ffn_b[0], ffn_tile)
        return x2, ret_new, cs0

    def ffn1(x2, o_t, conv_state0, tile, n_valid=None):
        return _ffn_layer(x2, conv_state0, w_up_b[1], conv_w[1], conv_b[1], w_dn_b[1],
                          ln_ffn_g[1], ln_ffn_b[1], tile, n_valid=n_valid,
                          mix=(o_t, w_ob, ln_mix_g[1], ln_mix_b[1]))

    zeros_ret = jnp.zeros((Bp, RET_HEADS, RET_DK, RET_DV), F32)
    zeros_cs = jnp.zeros((Bp, CONV_W - 1, D_FF), F32)
    x2p, ret_p, cs0_p = layer0(x_prompt, 0, zeros_ret, zeros_cs, RET_TILE, FFN_TILE)
    kp_, vp_, lft_p, kop, vop, qop = _fox_projection(
        x2p, jnp.zeros((Bp, H, LANES), F32), w_k, w_v, w_vt, w_qt, w_ft, b_f, FOX_TILE)
    ot_p = _prompt_attention(qop, kop, vop)
    y_p, cs1_p = ffn1(x2p, ot_p, zeros_cs, FFN_TILE)

    x2s, ret_s, cs0_s = layer0(x_sample, past, state_ret[0], state_ffn_conv[0], Ls, Ls)
    kc, vc, carry = _cache_operands(cache_k.reshape(Bs, past, HB), cache_v.reshape(Bs, past, HB),
                                    jnp.transpose(cache_logf.astype(F32), (0, 2, 1)), CACHE_TILE)
    Tpad = max(LANES, Ls)
    x2s_pad = jnp.pad(x2s, ((0, 0), (0, Tpad - Ls), (0, 0)))
    ks_, vs_, lft_s, kos, vos, qos = _fox_projection(x2s_pad, carry, w_k, w_v, w_vt, w_qt, w_ft, b_f, Tpad)
    ot_s = _sample_attention(qos, kc, vc, kos, vos)
    y_s_pad, cs1_s = ffn1(x2s_pad, ot_s, state_ffn_conv[1], Tpad, n_valid=Ls)

    def heads(a, L):
        return a[:, :L].reshape(a.shape[0], L, H, FOX_DH)

    return (y_p, y_s_pad[:, :Ls],
            ret_p[None], heads(kp_, Lp), heads(vp_, Lp), jnp.transpose(lft_p, (0, 2, 1)),
            jnp.stack([cs0_p, cs1_p]),
            ret_s[None], heads(ks_, Ls), heads(vs_, Ls), jnp.transpose(lft_s[:, :, :Ls], (0, 2, 1)),
            jnp.stack([cs0_s, cs1_s]))
```

```python
import functools
import math

import numpy as np
import jax
import jax.numpy as jnp
from jax import lax
from jax.experimental import pallas as pl
from jax.experimental.pallas import tpu as pltpu

F32 = jnp.float32
BF16 = jnp.bfloat16

D_MODEL = 1024
RET_DK = 256
RET_HEADS = D_MODEL // RET_DK
RET_DV = 2 * RET_DK
HK = RET_HEADS * RET_DK
HV = RET_HEADS * RET_DV
FOX_DH = 64
FOX_HEADS = D_MODEL // FOX_DH
HB = FOX_HEADS * FOX_DH
D_FF = 2816
CONV_W = 3
ROPE_BASE = 10000.0
LN_EPS = 1e-5
DEPTH = 2
ALPHA = (2 * DEPTH) ** 0.25
NEG_INF = -1e30
LOG2E = math.log2(math.e)

LANES = 128
AUG = 2 * FOX_DH
FF_CHUNK = 256
VMEM_LIMIT = 56 * 1024 * 1024

_COL_QC = (FOX_DH, FOX_DH + 1, FOX_DH + 2)
_COL_KC = (FOX_DH + 3, FOX_DH + 4, FOX_DH + 5)


def _dot(a, b):
    return jnp.dot(a, b, preferred_element_type=F32)


def _dot_nt(a, b):
    return lax.dot_general(a, b, (((1,), (1,)), ((), ())), preferred_element_type=F32)


def _dot_tn(a, b):
    return lax.dot_general(a, b, (((0,), (0,)), ((), ())), preferred_element_type=F32)


def _layernorm(z, g, b):
    mu = jnp.mean(z, axis=-1, keepdims=True)
    zc = z - mu
    var = jnp.mean(zc * zc, axis=-1, keepdims=True)
    return zc * lax.rsqrt(var + LN_EPS) * g + b


def _const_spec(shape):
    zeros = (0,) * len(shape)
    return pl.BlockSpec(shape, lambda *_: zeros, pipeline_mode=pl.Buffered(1))


def _params(sem, flags=None):
    return pltpu.CompilerParams(dimension_semantics=sem, vmem_limit_bytes=VMEM_LIMIT, flags=flags)


def _retention_kernel(x_ref, cos_ref, sin_ref, dmask_ref, qdec_ref, kdec_ref, s0_ref,
                      win_ref, wout_ref, lrg_ref, lrb_ref, lmg_ref, lmb_ref,
                      y_ref, s_ref, *, state_decay):
    @pl.when(pl.program_id(1) == 0)
    def _():
        s_ref[...] = s0_ref[...]

    x = x_ref[0]
    xb = x.astype(BF16)
    cos = cos_ref[...]
    sin = sin_ref[...]
    half = RET_DK // 2

    def rope(u):
        u1, u2 = u[:, :half], u[:, half:]
        return jnp.concatenate([u1 * cos - u2 * sin, u2 * cos + u1 * sin], axis=1)

    mix = None
    for h in range(RET_HEADS):
        q = rope(_dot(xb, win_ref[:, h * RET_DK:(h + 1) * RET_DK]))
        k = rope(_dot(xb, win_ref[:, HK + h * RET_DK:HK + (h + 1) * RET_DK])) * (RET_DK ** -0.5)
        v = _dot(xb, win_ref[:, 2 * HK + h * RET_DV:2 * HK + (h + 1) * RET_DV])
        g = _dot(xb, win_ref[:, 2 * HK + HV + h * RET_DV:2 * HK + HV + (h + 1) * RET_DV])
        vb = v.astype(BF16)
        scores = _dot_nt(q.astype(BF16), k.astype(BF16)) * dmask_ref[h]
        state = s_ref[0, h]
        o = _dot(scores.astype(BF16), vb) + _dot((q * qdec_ref[h]).astype(BF16), state.astype(BF16))
        s_ref[0, h] = state_decay[h] * state + _dot_tn((k * kdec_ref[h]).astype(BF16), vb)
        sl = slice(h * RET_DV, (h + 1) * RET_DV)
        o = _layernorm(o, lrg_ref[:, sl], lrb_ref[:, sl])
        part = _dot((jax.nn.silu(g) * o).astype(BF16), wout_ref[sl, :])
        mix = part if mix is None else mix + part
    y_ref[0] = _layernorm(ALPHA * x + mix, lmg_ref[...], lmb_ref[...])


def _retention_layer(x, pos0, state0, w_in, w_out, ln_ret_g, ln_ret_b, ln_mix_g, ln_mix_b, tile):
    B, L, D = x.shape
    T = min(tile, L)
    assert L % T == 0
    nt = L // T
    pos = (pos0 + jnp.arange(L, dtype=jnp.int32)).astype(F32)
    half = RET_DK // 2
    inv = 1.0 / (ROPE_BASE ** (jnp.arange(half, dtype=F32) / half))
    ang = pos[:, None] * inv[None, :]
    cos, sin = jnp.cos(ang), jnp.sin(ang)
    lg = np.log1p(-np.exp2(-5.0 - np.arange(RET_HEADS, dtype=np.float64)))
    idx = np.arange(T, dtype=np.float64)
    diff = idx[:, None] - idx[None, :]
    dmask = np.where(diff >= 0, np.exp(lg[:, None, None] * np.maximum(diff, 0.0)), 0.0)
    qdec = np.exp((idx[None, :] + 1.0) * lg[:, None])[..., None]
    kdec = np.exp((T - 1.0 - idx)[None, :] * lg[:, None])[..., None]
    state_decay = tuple(float(v) for v in np.exp(T * lg))

    vec = lambda a: a.reshape(1, -1).astype(F32)
    kern = functools.partial(_retention_kernel, state_decay=state_decay)
    y, s_new = pl.pallas_call(
        kern,
        grid=(B, nt),
        in_specs=[
            pl.BlockSpec((1, T, D), lambda b, t: (b, t, 0)),
            pl.BlockSpec((T, half), lambda b, t: (t, 0)),
            pl.BlockSpec((T, half), lambda b, t: (t, 0)),
            _const_spec((RET_HEADS, T, T)),
            _const_spec((RET_HEADS, T, 1)),
            _const_spec((RET_HEADS, T, 1)),
            pl.BlockSpec((1, RET_HEADS, RET_DK, RET_DV), lambda b, t: (b, 0, 0, 0)),
            _const_spec(w_in.shape),
            _const_spec(w_out.shape),
            _const_spec((1, HV)), _const_spec((1, HV)),
            _const_spec((1, D)), _const_spec((1, D)),
        ],
        out_specs=[
            pl.BlockSpec((1, T, D), lambda b, t: (b, t, 0)),
            pl.BlockSpec((1, RET_HEADS, RET_DK, RET_DV), lambda b, t: (b, 0, 0, 0)),
        ],
        out_shape=[
            jax.ShapeDtypeStruct((B, L, D), F32),
            jax.ShapeDtypeStruct((B, RET_HEADS, RET_DK, RET_DV), F32),
        ],
        compiler_params=_params(("arbitrary", "arbitrary")),
        name="retention_layer",
    )(x, cos, sin, jnp.asarray(dmask, F32), jnp.asarray(qdec, F32), jnp.asarray(kdec, F32),
      state0, w_in, w_out, vec(ln_ret_g), vec(ln_ret_b), vec(ln_mix_g), vec(ln_mix_b))
    return y, s_new


def _ffn_kernel(*refs, n_valid, with_mix):
    if with_mix:
        (x_ref, ot_ref, wo_ref, lmg_ref, lmb_ref, cs0_ref, wup_ref, cw_ref, cb_ref, wdn_ref,
         lfg_ref, lfb_ref, y_ref, cs_ref, hid_ref) = refs
    else:
        (x_ref, cs0_ref, wup_ref, cw_ref, cb_ref, wdn_ref,
         lfg_ref, lfb_ref, y_ref, cs_ref, hid_ref) = refs

    @pl.when(pl.program_id(1) == 0)
    def _():
        cs_ref[...] = cs0_ref[...]

    x = x_ref[0]
    if with_mix:
        o_t = ot_ref[0, :, 0].reshape(HB, x.shape[0])
        x = _layernorm(ALPHA * x + _dot_tn(o_t, wo_ref[...]), lmg_ref[...], lmb_ref[...])
    xb = x.astype(BF16)
    T = x.shape[0]
    row = lax.broadcasted_iota(jnp.int32, (T, FF_CHUNK), 0)
    for c in range(D_FF // FF_CHUNK):
        sl = slice(c * FF_CHUNK, (c + 1) * FF_CHUNK)
        val = _dot(xb, wup_ref[:, sl])
        a = _dot(xb, wup_ref[:, D_FF + c * FF_CHUNK:D_FF + (c + 1) * FF_CHUNK])
        prev2 = cs_ref[0, 0:1, sl]
        prev1 = cs_ref[0, 1:2, sl]
        a1 = jnp.where(row == 0, prev1, pltpu.roll(a, 1, 0))
        a2 = jnp.where(row == 0, prev2, jnp.where(row == 1, prev1, pltpu.roll(a, 2, 0)))
        conv = cb_ref[:, sl] + cw_ref[0:1, sl] * a2 + cw_ref[1:2, sl] * a1 + cw_ref[2:3, sl] * a
        gelu = 0.5 * conv * (1.0 + lax.erf(conv * (2.0 ** -0.5)))
        hid_ref[:, sl] = (gelu * val).astype(BF16)
        cs_ref[0, :, sl] = a[n_valid - 2:n_valid, :]
    f = _dot(hid_ref[...], wdn_ref[...])
    y_ref[0] = _layernorm(ALPHA * x + f, lfg_ref[...], lfb_ref[...])


def _ffn_layer(x, conv_state0, w_up, conv_w, conv_b, w_down, ln_g, ln_b, tile, n_valid=None,
               mix=None):
    B, L, D = x.shape
    T = min(tile, L)
    assert L % T == 0
    nt = L // T
    if n_valid is None:
        n_valid = T
    else:
        assert nt == 1
    vec = lambda a: a.reshape(1, -1).astype(F32)
    row_spec = pl.BlockSpec((1, T, D), lambda b, t: (b, t, 0))
    cs_spec = pl.BlockSpec((1, CONV_W - 1, D_FF), lambda b, t: (b, 0, 0))
    args, specs = [x], [row_spec]
    if mix is not None:
        o_t, w_o, lmg, lmb = mix
        args += [o_t, w_o, vec(lmg), vec(lmb)]
        per_blk = o_t.shape[4] // T
        assert o_t.shape[4] % T == 0 and o_t.shape[2] * per_blk == nt
        specs += [pl.BlockSpec((1, FOX_HEADS, 1, FOX_DH, T),
                               lambda b, t: (b, 0, t // per_blk, 0, t % per_blk)),
                  _const_spec(w_o.shape),
                  _const_spec((1, D)), _const_spec((1, D))]
    args += [conv_state0, w_up, conv_w, vec(conv_b), w_down, vec(ln_g), vec(ln_b)]
    specs += [cs_spec, _const_spec(w_up.shape), _const_spec(conv_w.shape), _const_spec((1, D_FF)),
              _const_spec(w_down.shape), _const_spec((1, D)), _const_spec((1, D))]
    kern = functools.partial(_ffn_kernel, n_valid=n_valid, with_mix=mix is not None)
    y, cs = pl.pallas_call(
        kern,
        grid=(B, nt),
        in_specs=specs,
        out_specs=[row_spec, cs_spec],
        out_shape=[jax.ShapeDtypeStruct((B, L, D), F32),
                   jax.ShapeDtypeStruct((B, CONV_W - 1, D_FF), F32)],
        scratch_shapes=[pltpu.VMEM((T, D_FF), BF16)],
        compiler_params=_params(("arbitrary", "arbitrary")),
        name="ffn_mix_layer" if mix is not None else "ffn_layer",
    )(*args)
    return y, cs


def _cumsum_lanes(x):
    n = x.shape[1]
    lane = lax.broadcasted_iota(jnp.int32, x.shape, 1)
    s = 1
    while s < n:
        x = x + jnp.where(lane >= s, pltpu.roll(x, s, 1), 0.0)
        s *= 2
    return x


def _split3(c):
    hi = c.astype(BF16).astype(F32)
    r = c - hi
    mid = r.astype(BF16).astype(F32)
    lo = (r - mid).astype(BF16).astype(F32)
    return hi, mid, lo


def _advance_cumsum(logf_t, carry_ref):
    c_t = carry_ref[:, 0:1] + _cumsum_lanes(logf_t)
    T = logf_t.shape[1]
    carry_ref[...] = jnp.broadcast_to(c_t[:, T - 1:T], carry_ref.shape)
    return c_t


def _store_k_operand(k, c_parts, place_ref, kp_ref):
    hi, mid, lo = c_parts
    T = k.shape[0]
    cp_t = jnp.concatenate(
        [-hi, -mid, -lo, jnp.ones((8, T), F32), jnp.zeros((LANES - 3 * FOX_HEADS - 8, T), F32)], axis=0)
    cp = cp_t.T.astype(BF16)
    aug = _dot(cp, place_ref[...])
    lane = lax.broadcasted_iota(jnp.int32, (T, AUG), 1)
    for p in range(FOX_HEADS // 2):
        slab = k[:, p * AUG:(p + 1) * AUG]
        h0, h1 = 2 * p, 2 * p + 1
        kp_ref[0, h0] = jnp.where(lane < FOX_DH, slab, aug[:, h0 * AUG:(h0 + 1) * AUG]).astype(BF16)
        kp_ref[0, h1] = jnp.where(lane < FOX_DH, pltpu.roll(slab, FOX_DH, 1),
                                  aug[:, h1 * AUG:(h1 + 1) * AUG]).astype(BF16)


def _store_v_operand(v_t, vp_ref):
    T = v_t.shape[1]
    sub = lax.broadcasted_iota(jnp.int32, (FOX_DH, T), 0)
    ones_row = jnp.where(sub == 0, 1.0, 0.0).astype(BF16)
    for h in range(FOX_HEADS):
        vp_ref[0, h, 0, 0:FOX_DH, :] = v_t[h * FOX_DH:(h + 1) * FOX_DH].astype(BF16)
        vp_ref[0, h, 0, FOX_DH:AUG, :] = ones_row


def _store_q_operand(q_t, c_parts, qp_ref):
    hi, mid, lo = c_parts
    T = q_t.shape[1]
    sub = lax.broadcasted_iota(jnp.int32, (FOX_DH, T), 0)
    for h in range(FOX_HEADS):
        aug = jnp.where(sub == 0, hi[h:h + 1],
              jnp.where(sub == 1, mid[h:h + 1],
              jnp.where(sub == 2, lo[h:h + 1],
              jnp.where(sub < 6, 1.0, 0.0))))
        qp_ref[0, h, 0, 0:FOX_DH, :] = q_t[h * FOX_DH:(h + 1) * FOX_DH].astype(BF16)
        qp_ref[0, h, 0, FOX_DH:AUG, :] = aug.astype(BF16)


def _placement_matrix():
    p = np.zeros((LANES, FOX_HEADS * AUG), np.float32)
    for h in range(FOX_HEADS):
        for part in range(3):
            p[part * FOX_HEADS + h, h * AUG + _COL_KC[part]] = 1.0
            p[3 * FOX_HEADS, h * AUG + _COL_QC[part]] = 1.0
    return jnp.asarray(p, BF16)


def _fox_proj_kernel(x_ref, carry0_ref, wk_ref, wv_ref, wvt_ref, wqt_ref, wft_ref, bf_ref, place_ref,
                     k_ref, v_ref, lf_ref, kp_ref, vp_ref, qp_ref, carry_ref):
    @pl.when(pl.program_id(1) == 0)
    def _():
        carry_ref[...] = carry0_ref[0]

    xb = x_ref[0].astype(BF16)
    k = _dot(xb, wk_ref[...])
    k_ref[0] = k
    v_ref[0] = _dot(xb, wv_ref[...])
    logf_t = jax.nn.log_sigmoid(_dot_nt(wft_ref[...], xb) + bf_ref[...])
    lf_ref[0] = logf_t
    c_parts = _split3(_advance_cumsum(logf_t, carry_ref) * LOG2E)
    _store_k_operand(k, c_parts, place_ref, kp_ref)
    _store_v_operand(_dot_nt(wvt_ref[...], xb), vp_ref)
    _store_q_operand(_dot_nt(wqt_ref[...], xb), c_parts, qp_ref)


def _fox_projection(x, carry0, w_k, w_v, w_vt, w_qt, w_ft, b_f, tile):
    B, L, D = x.shape
    T = min(tile, L)
    assert L % T == 0 and T % LANES == 0
    nt = L // T
    H = FOX_HEADS
    return pl.pallas_call(
        _fox_proj_kernel,
        grid=(B, nt),
        in_specs=[
            pl.BlockSpec((1, T, D), lambda b, t: (b, t, 0)),
            pl.BlockSpec((1, H, LANES), lambda b, t: (b, 0, 0)),
            _const_spec(w_k.shape), _const_spec(w_v.shape), _const_spec(w_vt.shape),
            _const_spec(w_qt.shape), _const_spec(w_ft.shape), _const_spec((H, 1)),
            _const_spec((LANES, H * AUG)),
        ],
        out_specs=[
            pl.BlockSpec((1, T, HB), lambda b, t: (b, t, 0)),
            pl.BlockSpec((1, T, HB), lambda b, t: (b, t, 0)),
            pl.BlockSpec((1, H, T), lambda b, t: (b, 0, t)),
            pl.BlockSpec((1, H, T, AUG), lambda b, t: (b, 0, t, 0)),
            pl.BlockSpec((1, H, 1, AUG, T), lambda b, t: (b, 0, t, 0, 0)),
            pl.BlockSpec((1, H, 1, AUG, T), lambda b, t: (b, 0, t, 0, 0)),
        ],
        out_shape=[
            jax.ShapeDtypeStruct((B, L, HB), F32),
            jax.ShapeDtypeStruct((B, L, HB), F32),
            jax.ShapeDtypeStruct((B, H, L), F32),
            jax.ShapeDtypeStruct((B, H, L, AUG), BF16),
            jax.ShapeDtypeStruct((B, H, nt, AUG, T), BF16),
            jax.ShapeDtypeStruct((B, H, nt, AUG, T), BF16),
        ],
        scratch_shapes=[pltpu.VMEM((H, LANES), F32)],
        compiler_params=_params(("arbitrary", "arbitrary")),
        name="fox_projection",
    )(x, carry0, w_k, w_v, w_vt, w_qt, w_ft, b_f.reshape(H, 1).astype(F32), _placement_matrix())


def _cache_kernel(k_ref, v_ref, lf_ref, eye_ref, place_ref, kp_ref, vp_ref, carry_ref):
    @pl.when(pl.program_id(1) == 0)
    def _():
        carry_ref[...] = jnp.zeros_like(carry_ref)

    c_parts = _split3(_advance_cumsum(lf_ref[0], carry_ref.at[0]) * LOG2E)
    _store_k_operand(k_ref[0], c_parts, place_ref, kp_ref)
    _store_v_operand(_dot_nt(eye_ref[...], v_ref[0].astype(BF16)), vp_ref)


def _cache_operands(cache_k, cache_v, cache_logf_t, tile):
    B, P, _ = cache_k.shape
    T = min(tile, P)
    assert P % T == 0 and T % LANES == 0
    nt = P // T
    H = FOX_HEADS
    return pl.pallas_call(
        _cache_kernel,
        grid=(B, nt),
        in_specs=[
            pl.BlockSpec((1, T, HB), lambda b, t: (b, t, 0)),
            pl.BlockSpec((1, T, HB), lambda b, t: (b, t, 0)),
            pl.BlockSpec((1, H, T), lambda b, t: (b, 0, t)),
            _const_spec((HB, HB)),
            _const_spec((LANES, H * AUG)),
        ],
        out_specs=[
            pl.BlockSpec((1, H, T, AUG), lambda b, t: (b, 0, t, 0)),
            pl.BlockSpec((1, H, 1, AUG, T), lambda b, t: (b, 0, t, 0, 0)),
            pl.BlockSpec((1, H, LANES), lambda b, t: (b, 0, 0)),
        ],
        out_shape=[
            jax.ShapeDtypeStruct((B, H, P, AUG), BF16),
            jax.ShapeDtypeStruct((B, H, nt, AUG, T), BF16),
            jax.ShapeDtypeStruct((B, H, LANES), F32),
        ],
        compiler_params=_params(("arbitrary", "arbitrary")),
        name="cache_operands",
    )(cache_k, cache_v, cache_logf_t, jnp.eye(HB, dtype=BF16), _placement_matrix())


def _finish_attention(acc):
    return (acc[0:FOX_DH] / acc[FOX_DH:FOX_DH + 1]).astype(BF16)


MXU_TILE = 256


def _prompt_attn_kernel(ti_ref, tj_ref, q_ref, k_ref, v_ref, o_ref,
                        sa_ref, sb_ref, bma_ref, bmb_ref, m_ref, acc_ref,
                        *, n_masked, n_plain, unroll, tiled):
    nq, _, T = q_ref.shape[2:]
    col_tiles = ([slice(c * MXU_TILE, (c + 1) * MXU_TILE) for c in range(T // MXU_TILE)]
                 if tiled else [slice(0, T)])

    def k_block(j):
        return k_ref[0, 0, pl.ds(pl.multiple_of(j * T, T), T), :]

    def init(i, carry):
        m_ref[i] = jnp.full((1, T), NEG_INF, F32)
        acc_ref[i] = jnp.zeros((AUG, T), F32)
        return carry

    lax.fori_loop(0, nq + 1, init, 0)

    bufs = ((sa_ref, bma_ref), (sb_ref, bmb_ref))

    def produce(t, slot, masked=False):
        s_ref, bm_ref = bufs[slot]
        k_blk = k_block(tj_ref[t])
        qi = jnp.minimum(ti_ref[t], nq - 1)
        for cs in col_tiles:
            s = _dot(k_blk, q_ref[0, 0, qi, :, cs])
            if masked:
                w = cs.stop - cs.start
                key = lax.broadcasted_iota(jnp.int32, (T, w), 0)
                qry = lax.broadcasted_iota(jnp.int32, (T, w), 1) + cs.start
                s = jnp.where(key <= qry, s, NEG_INF)
            s_ref[:, cs] = s
            bm_ref[:, cs] = jnp.max(s, axis=0, keepdims=True)

    def consume(t, slot):
        s_ref, bm_ref = bufs[slot]
        i = ti_ref[t]
        j = tj_ref[t]
        for cs in col_tiles:
            m_prev = m_ref[i, :, cs]
            m_new = jnp.maximum(m_prev, bm_ref[:, cs])
            p = jnp.exp2(s_ref[:, cs] - m_new).astype(BF16)
            acc_ref[i, :, cs] = jnp.exp2(m_prev - m_new) * acc_ref[i, :, cs] + _dot(v_ref[0, 0, j], p)
            m_ref[i, :, cs] = m_new

    produce(0, 0)

    def trip(u, carry, mask_last):
        t0 = unroll * u
        for k in range(unroll):
            produce(t0 + k + 1, (k + 1) % 2, masked=mask_last and k == unroll - 1)
            consume(t0 + k, k % 2)
        return carry

    lax.fori_loop(0, n_masked, functools.partial(trip, mask_last=True), 0)
    lax.fori_loop(n_masked, n_masked + n_plain, functools.partial(trip, mask_last=False), 0)

    def finish(i, carry):
        o_ref[0, 0, i] = _finish_attention(acc_ref[i])
        return carry

    lax.fori_loop(0, nq, finish, 0)


def _attention_tasks(nq, unroll):
    pad = (nq, 0)
    off = [(i, j) for i in range(nq) for j in range(i)]
    n_trips = max(nq + 1, -(-(len(off) + nq) // unroll))
    tasks = [None] * (n_trips * unroll + 1)
    for i in range(nq):
        tasks[unroll * (i + 1)] = (i, i)
    rest = iter(off)
    tasks = [t if t is not None else next(rest, pad) for t in tasks]
    ti = jnp.asarray([t[0] for t in tasks], jnp.int32)
    tj = jnp.asarray([t[1] for t in tasks], jnp.int32)
    return ti, tj, nq, n_trips - nq


def _prompt_attention(qp, kp, vp, b0, nb, h0, nh, *, unroll, tiled, name):
    _, _, nq, _, T = qp.shape
    L = nq * T
    assert unroll % 2 == 0 and T % MXU_TILE == 0
    ti, tj, n_masked, n_plain = _attention_tasks(nq, unroll)
    grid_spec = pltpu.PrefetchScalarGridSpec(
        num_scalar_prefetch=2,
        grid=(nb, nh),
        in_specs=[
            pl.BlockSpec((1, 1, nq, AUG, T), lambda b, h, ti, tj: (b0 + b, h0 + h, 0, 0, 0)),
            pl.BlockSpec((1, 1, L, AUG), lambda b, h, ti, tj: (b0 + b, h0 + h, 0, 0)),
            pl.BlockSpec((1, 1, nq, AUG, T), lambda b, h, ti, tj: (b0 + b, h0 + h, 0, 0, 0)),
        ],
        out_specs=pl.BlockSpec((1, 1, nq, FOX_DH, T), lambda b, h, ti, tj: (b, h, 0, 0, 0)),
        scratch_shapes=[
            pltpu.VMEM((T, T), F32), pltpu.VMEM((T, T), F32),
            pltpu.VMEM((1, T), F32), pltpu.VMEM((1, T), F32),
            pltpu.VMEM((nq + 1, 1, T), F32),
            pltpu.VMEM((nq + 1, AUG, T), F32),
        ],
    )
    kern = functools.partial(_prompt_attn_kernel, n_masked=n_masked, n_plain=n_plain,
                             unroll=unroll, tiled=tiled)
    return pl.pallas_call(
        kern,
        grid_spec=grid_spec,
        out_shape=jax.ShapeDtypeStruct((nb, nh, nq, FOX_DH, T), BF16),
        compiler_params=_params(("arbitrary", "arbitrary")),
        name=name,
    )(ti, tj, qp, kp, vp)


def _sample_attn_kernel(q_ref, kc_ref, vc_ref, kn_ref, vn_ref, o_ref):
    q_t = q_ref[0, 0, 0]
    T = q_t.shape[1]
    n_c, _, t_c = vc_ref.shape[2:]
    s_c = _dot(kc_ref[0, 0], q_t)
    s_n = _dot(kn_ref[0, 0], q_t)
    key = lax.broadcasted_iota(jnp.int32, (T, T), 0)
    qry = lax.broadcasted_iota(jnp.int32, (T, T), 1)
    s_n = jnp.where(key <= qry, s_n, NEG_INF)
    m = jnp.maximum(jnp.max(s_c, axis=0, keepdims=True), jnp.max(s_n, axis=0, keepdims=True))
    p_c = jnp.exp2(s_c - m).astype(BF16)
    acc = _dot(vn_ref[0, 0, 0], jnp.exp2(s_n - m).astype(BF16))
    for j in range(n_c):
        acc = acc + _dot(vc_ref[0, 0, j], p_c[j * t_c:(j + 1) * t_c])
    o_ref[0, 0, 0] = _finish_attention(acc)


def _sample_attention(qp, kp_cache, vp_cache, kp_new, vp_new):
    B, H, _, _, T = qp.shape
    P = kp_cache.shape[2]
    n_c, _, t_c = vp_cache.shape[2:]
    return pl.pallas_call(
        _sample_attn_kernel,
        grid=(B, H),
        in_specs=[
            pl.BlockSpec((1, 1, 1, AUG, T), lambda b, h: (b, h, 0, 0, 0)),
            pl.BlockSpec((1, 1, P, AUG), lambda b, h: (b, h, 0, 0)),
            pl.BlockSpec((1, 1, n_c, AUG, t_c), lambda b, h: (b, h, 0, 0, 0)),
            pl.BlockSpec((1, 1, T, AUG), lambda b, h: (b, h, 0, 0)),
            pl.BlockSpec((1, 1, 1, AUG, T), lambda b, h: (b, h, 0, 0, 0)),
        ],
        out_specs=pl.BlockSpec((1, 1, 1, FOX_DH, T), lambda b, h: (b, h, 0, 0, 0)),
        out_shape=jax.ShapeDtypeStruct((B, H, 1, FOX_DH, T), BF16),
        compiler_params=_params(("arbitrary", "arbitrary")),
        name="sample_attention",
    )(qp, kp_cache, vp_cache, kp_new, vp_new)


RET_TILE = 512
FFN_TILE = 512
FOX_TILE = 512
CACHE_TILE = 512


def kernel(x_prompt, x_sample, cache_k, cache_v, cache_logf, state_ret, state_ffn_conv,
           w_in_a, ln_ret_g, ln_ret_b, w_out_a, w_kvf, b_f, w_q_b, w_out_b,
           ln_mix_g, ln_mix_b, w_up, conv_w, conv_b, w_down, ln_ffn_g, ln_ffn_b):
    Bp, Lp, _ = x_prompt.shape
    Bs, Ls, _ = x_sample.shape
    past = cache_k.shape[1]
    H = FOX_HEADS

    w_in = w_in_a[0].astype(BF16)
    w_oa = w_out_a[0].astype(BF16)
    w_k = w_kvf[:, :HB].astype(BF16)
    w_v = w_kvf[:, HB:2 * HB].astype(BF16)
    w_vt = w_v.T
    w_ft = w_kvf[:, 2 * HB:].T.astype(BF16)
    w_qt = (w_q_b[0] * (FOX_DH ** -0.5 * LOG2E)).T.astype(BF16)
    w_ob = w_out_b[0].astype(BF16)
    w_up_b = w_up.astype(BF16)
    w_dn_b = w_down.astype(BF16)

    def layer0(x, pos0, ret_state0, conv_state0, ret_tile, ffn_tile):
        x1, ret_new = _retention_layer(x, pos0, ret_state0, w_in, w_oa, ln_ret_g[0], ln_ret_b[0],
                                       ln_mix_g[0], ln_mix_b[0], ret_tile)
        x2, cs0 = _ffn_layer(x1, conv_state0, w_up_b[0], conv_w[0], conv_b[0], w_dn_b[0],
                             ln_ffn_g[0], ln_ffn_b[0], ffn_tile)
        return x2, ret_new, cs0

    def ffn1(x2, o_t, conv_state0, tile, n_valid=None):
        return _ffn_layer(x2, conv_state0, w_up_b[1], conv_w[1], conv_b[1], w_dn_b[1],
                          ln_ffn_g[1], ln_ffn_b[1], tile, n_valid=n_valid,
                          mix=(o_t, w_ob, ln_mix_g[1], ln_mix_b[1]))

    zeros_ret = jnp.zeros((Bp, RET_HEADS, RET_DK, RET_DV), F32)
    zeros_cs = jnp.zeros((Bp, CONV_W - 1, D_FF), F32)
    x2p, ret_p, cs0_p = layer0(x_prompt, 0, zeros_ret, zeros_cs, RET_TILE, FFN_TILE)
    kp_, vp_, lft_p, kop, vop, qop = _fox_projection(
        x2p, jnp.zeros((Bp, H, LANES), F32), w_k, w_v, w_vt, w_qt, w_ft, b_f, FOX_TILE)
    assert Bp == 2 and H % 2 == 0
    hh = H // 2
    shapes = ((("u8", 8, False), ("u8t", 8, True)), (("u16", 16, False), ("u4", 4, False)))
    ot_p = jnp.concatenate([
        jnp.concatenate([
            _prompt_attention(qop, kop, vop, b, 1, k * hh, hh, unroll=unroll, tiled=tiled,
                              name="prompt_attention_" + tag)
            for k, (tag, unroll, tiled) in enumerate(row)], axis=1)
        for b, row in enumerate(shapes)], axis=0)
    y_p, cs1_p = ffn1(x2p, ot_p, zeros_cs, FFN_TILE)

    x2s, ret_s, cs0_s = layer0(x_sample, past, state_ret[0], state_ffn_conv[0], Ls, Ls)
    kc, vc, carry = _cache_operands(cache_k.reshape(Bs, past, HB), cache_v.reshape(Bs, past, HB),
                                    jnp.transpose(cache_logf.astype(F32), (0, 2, 1)), CACHE_TILE)
    Tpad = max(LANES, Ls)
    x2s_pad = jnp.pad(x2s, ((0, 0), (0, Tpad - Ls), (0, 0)))
    ks_, vs_, lft_s, kos, vos, qos = _fox_projection(x2s_pad, carry, w_k, w_v, w_vt, w_qt, w_ft, b_f, Tpad)
    ot_s = _sample_attention(qos, kc, vc, kos, vos)
    y_s_pad, cs1_s = ffn1(x2s_pad, ot_s, state_ffn_conv[1], Tpad, n_valid=Ls)

    def heads(a, L):
        return a[:, :L].reshape(a.shape[0], L, H, FOX_DH)

    return (y_p, y_s_pad[:, :Ls],
            ret_p[None], heads(kp_, Lp), heads(vp_, Lp), jnp.transpose(lft_p, (0, 2, 1)),
            jnp.stack([cs0_p, cs1_p]),
            ret_s[None], heads(ks_, Ls), heads(vs_, Ls), jnp.transpose(lft_s[:, :, :Ls], (0, 2, 1)),
            jnp.stack([cs0_s, cs1_s]))
```

```python
import functools
import math

import numpy as np
import jax
import jax.numpy as jnp
from jax import lax
from jax.experimental import pallas as pl
from jax.experimental.pallas import tpu as pltpu

F32 = jnp.float32
BF16 = jnp.bfloat16

D_MODEL = 1024
RET_DK = 256
RET_HEADS = D_MODEL // RET_DK
RET_DV = 2 * RET_DK
HK = RET_HEADS * RET_DK
HV = RET_HEADS * RET_DV
FOX_DH = 64
FOX_HEADS = D_MODEL // FOX_DH
HB = FOX_HEADS * FOX_DH
D_FF = 2816
CONV_W = 3
ROPE_BASE = 10000.0
LN_EPS = 1e-5
DEPTH = 2
ALPHA = (2 * DEPTH) ** 0.25
NEG_INF = -1e30
LOG2E = math.log2(math.e)

LANES = 128
AUG = 2 * FOX_DH
FF_CHUNK = 256
VMEM_LIMIT = 56 * 1024 * 1024

_COL_QC = (FOX_DH, FOX_DH + 1, FOX_DH + 2)
_COL_KC = (FOX_DH + 3, FOX_DH + 4, FOX_DH + 5)


def _dot(a, b):
    return jnp.dot(a, b, preferred_element_type=F32)


def _dot_nt(a, b):
    return lax.dot_general(a, b, (((1,), (1,)), ((), ())), preferred_element_type=F32)


def _dot_tn(a, b):
    return lax.dot_general(a, b, (((0,), (0,)), ((), ())), preferred_element_type=F32)


def _layernorm(z, g, b):
    mu = jnp.mean(z, axis=-1, keepdims=True)
    zc = z - mu
    var = jnp.mean(zc * zc, axis=-1, keepdims=True)
    return zc * lax.rsqrt(var + LN_EPS) * g + b


def _const_spec(shape):
    zeros = (0,) * len(shape)
    return pl.BlockSpec(shape, lambda *_: zeros, pipeline_mode=pl.Buffered(1))


def _params(sem, flags=None):
    return pltpu.CompilerParams(dimension_semantics=sem, vmem_limit_bytes=VMEM_LIMIT, flags=flags)


def _retention_kernel(x_ref, cos_ref, sin_ref, dmask_ref, qdec_ref, kdec_ref, s0_ref,
                      win_ref, wout_ref, lrg_ref, lrb_ref, lmg_ref, lmb_ref,
                      y_ref, s_ref, *, state_decay):
    @pl.when(pl.program_id(1) == 0)
    def _():
        s_ref[...] = s0_ref[...]

    x = x_ref[0]
    xb = x.astype(BF16)
    cos = cos_ref[...]
    sin = sin_ref[...]
    half = RET_DK // 2

    def rope(u):
        u1, u2 = u[:, :half], u[:, half:]
        return jnp.concatenate([u1 * cos - u2 * sin, u2 * cos + u1 * sin], axis=1)

    mix = None
    for h in range(RET_HEADS):
        q = rope(_dot(xb, win_ref[:, h * RET_DK:(h + 1) * RET_DK]))
        k = rope(_dot(xb, win_ref[:, HK + h * RET_DK:HK + (h + 1) * RET_DK])) * (RET_DK ** -0.5)
        v = _dot(xb, win_ref[:, 2 * HK + h * RET_DV:2 * HK + (h + 1) * RET_DV])
        g = _dot(xb, win_ref[:, 2 * HK + HV + h * RET_DV:2 * HK + HV + (h + 1) * RET_DV])
        vb = v.astype(BF16)
        scores = _dot_nt(q.astype(BF16), k.astype(BF16)) * dmask_ref[h]
        state = s_ref[0, h]
        o = _dot(scores.astype(BF16), vb) + _dot((q * qdec_ref[h]).astype(BF16), state.astype(BF16))
        s_ref[0, h] = state_decay[h] * state + _dot_tn((k * kdec_ref[h]).astype(BF16), vb)
        sl = slice(h * RET_DV, (h + 1) * RET_DV)
        o = _layernorm(o, lrg_ref[:, sl], lrb_ref[:, sl])
        part = _dot((jax.nn.silu(g) * o).astype(BF16), wout_ref[sl, :])
        mix = part if mix is None else mix + part
    y_ref[0] = _layernorm(ALPHA * x + mix, lmg_ref[...], lmb_ref[...])


def _retention_layer(x, pos0, state0, w_in, w_out, ln_ret_g, ln_ret_b, ln_mix_g, ln_mix_b, tile):
    B, L, D = x.shape
    T = min(tile, L)
    assert L % T == 0
    nt = L // T
    pos = (pos0 + jnp.arange(L, dtype=jnp.int32)).astype(F32)
    half = RET_DK // 2
    inv = 1.0 / (ROPE_BASE ** (jnp.arange(half, dtype=F32) / half))
    ang = pos[:, None] * inv[None, :]
    cos, sin = jnp.cos(ang), jnp.sin(ang)
    lg = np.log1p(-np.exp2(-5.0 - np.arange(RET_HEADS, dtype=np.float64)))
    idx = np.arange(T, dtype=np.float64)
    diff = idx[:, None] - idx[None, :]
    dmask = np.where(diff >= 0, np.exp(lg[:, None, None] * np.maximum(diff, 0.0)), 0.0)
    qdec = np.exp((idx[None, :] + 1.0) * lg[:, None])[..., None]
    kdec = np.exp((T - 1.0 - idx)[None, :] * lg[:, None])[..., None]
    state_decay = tuple(float(v) for v in np.exp(T * lg))

    vec = lambda a: a.reshape(1, -1).astype(F32)
    kern = functools.partial(_retention_kernel, state_decay=state_decay)
    y, s_new = pl.pallas_call(
        kern,
        grid=(B, nt),
        in_specs=[
            pl.BlockSpec((1, T, D), lambda b, t: (b, t, 0)),
            pl.BlockSpec((T, half), lambda b, t: (t, 0)),
            pl.BlockSpec((T, half), lambda b, t: (t, 0)),
            _const_spec((RET_HEADS, T, T)),
            _const_spec((RET_HEADS, T, 1)),
            _const_spec((RET_HEADS, T, 1)),
            pl.BlockSpec((1, RET_HEADS, RET_DK, RET_DV), lambda b, t: (b, 0, 0, 0)),
            _const_spec(w_in.shape),
            _const_spec(w_out.shape),
            _const_spec((1, HV)), _const_spec((1, HV)),
            _const_spec((1, D)), _const_spec((1, D)),
        ],
        out_specs=[
            pl.BlockSpec((1, T, D), lambda b, t: (b, t, 0)),
            pl.BlockSpec((1, RET_HEADS, RET_DK, RET_DV), lambda b, t: (b, 0, 0, 0)),
        ],
        out_shape=[
            jax.ShapeDtypeStruct((B, L, D), F32),
            jax.ShapeDtypeStruct((B, RET_HEADS, RET_DK, RET_DV), F32),
        ],
        compiler_params=_params(("arbitrary", "arbitrary")),
        name="retention_layer",
    )(x, cos, sin, jnp.asarray(dmask, F32), jnp.asarray(qdec, F32), jnp.asarray(kdec, F32),
      state0, w_in, w_out, vec(ln_ret_g), vec(ln_ret_b), vec(ln_mix_g), vec(ln_mix_b))
    return y, s_new


def _ffn_kernel(*refs, n_valid, with_mix):
    if with_mix:
        (x_ref, ot_ref, wo_ref, lmg_ref, lmb_ref, cs0_ref, wup_ref, cw_ref, cb_ref, wdn_ref,
         lfg_ref, lfb_ref, y_ref, cs_ref, hid_ref) = refs
    else:
        (x_ref, cs0_ref, wup_ref, cw_ref, cb_ref, wdn_ref,
         lfg_ref, lfb_ref, y_ref, cs_ref, hid_ref) = refs

    @pl.when(pl.program_id(1) == 0)
    def _():
        cs_ref[...] = cs0_ref[...]

    x = x_ref[0]
    if with_mix:
        o_t = ot_ref[0, :, 0].reshape(HB, x.shape[0])
        x = _layernorm(ALPHA * x + _dot_tn(o_t, wo_ref[...]), lmg_ref[...], lmb_ref[...])
    xb = x.astype(BF16)
    T = x.shape[0]
    row = lax.broadcasted_iota(jnp.int32, (T, FF_CHUNK), 0)
    for c in range(D_FF // FF_CHUNK):
        sl = slice(c * FF_CHUNK, (c + 1) * FF_CHUNK)
        val = _dot(xb, wup_ref[:, sl])
        a = _dot(xb, wup_ref[:, D_FF + c * FF_CHUNK:D_FF + (c + 1) * FF_CHUNK])
        prev2 = cs_ref[0, 0:1, sl]
        prev1 = cs_ref[0, 1:2, sl]
        a1 = jnp.where(row == 0, prev1, pltpu.roll(a, 1, 0))
        a2 = jnp.where(row == 0, prev2, jnp.where(row == 1, prev1, pltpu.roll(a, 2, 0)))
        conv = cb_ref[:, sl] + cw_ref[0:1, sl] * a2 + cw_ref[1:2, sl] * a1 + cw_ref[2:3, sl] * a
        gelu = 0.5 * conv * (1.0 + lax.erf(conv * (2.0 ** -0.5)))
        hid_ref[:, sl] = (gelu * val).astype(BF16)
        cs_ref[0, :, sl] = a[n_valid - 2:n_valid, :]
    f = _dot(hid_ref[...], wdn_ref[...])
    y_ref[0] = _layernorm(ALPHA * x + f, lfg_ref[...], lfb_ref[...])


def _ffn_layer(x, conv_state0, w_up, conv_w, conv_b, w_down, ln_g, ln_b, tile, n_valid=None,
               mix=None):
    B, L, D = x.shape
    T = min(tile, L)
    assert L % T == 0
    nt = L // T
    if n_valid is None:
        n_valid = T
    else:
        assert nt == 1
    vec = lambda a: a.reshape(1, -1).astype(F32)
    row_spec = pl.BlockSpec((1, T, D), lambda b, t: (b, t, 0))
    cs_spec = pl.BlockSpec((1, CONV_W - 1, D_FF), lambda b, t: (b, 0, 0))
    args, specs = [x], [row_spec]
    if mix is not None:
        o_t, w_o, lmg, lmb = mix
        args += [o_t, w_o, vec(lmg), vec(lmb)]
        per_blk = o_t.shape[4] // T
        assert o_t.shape[4] % T == 0 and o_t.shape[2] * per_blk == nt
        specs += [pl.BlockSpec((1, FOX_HEADS, 1, FOX_DH, T),
                               lambda b, t: (b, 0, t // per_blk, 0, t % per_blk)),
                  _const_spec(w_o.shape),
                  _const_spec((1, D)), _const_spec((1, D))]
    args += [conv_state0, w_up, conv_w, vec(conv_b), w_down, vec(ln_g), vec(ln_b)]
    specs += [cs_spec, _const_spec(w_up.shape), _const_spec(conv_w.shape), _const_spec((1, D_FF)),
              _const_spec(w_down.shape), _const_spec((1, D)), _const_spec((1, D))]
    kern = functools.partial(_ffn_kernel, n_valid=n_valid, with_mix=mix is not None)
    y, cs = pl.pallas_call(
        kern,
        grid=(B, nt),
        in_specs=specs,
        out_specs=[row_spec, cs_spec],
        out_shape=[jax.ShapeDtypeStruct((B, L, D), F32),
                   jax.ShapeDtypeStruct((B, CONV_W - 1, D_FF), F32)],
        scratch_shapes=[pltpu.VMEM((T, D_FF), BF16)],
        compiler_params=_params(("arbitrary", "arbitrary")),
        name="ffn_mix_layer" if mix is not None else "ffn_layer",
    )(*args)
    return y, cs


def _cumsum_lanes(x):
    n = x.shape[1]
    lane = lax.broadcasted_iota(jnp.int32, x.shape, 1)
    s = 1
    while s < n:
        x = x + jnp.where(lane >= s, pltpu.roll(x, s, 1), 0.0)
        s *= 2
    return x


def _split3(c):
    hi = c.astype(BF16).astype(F32)
    r = c - hi
    mid = r.astype(BF16).astype(F32)
    lo = (r - mid).astype(BF16).astype(F32)
    return hi, mid, lo


def _advance_cumsum(logf_t, carry_ref):
    c_t = carry_ref[:, 0:1] + _cumsum_lanes(logf_t)
    T = logf_t.shape[1]
    carry_ref[...] = jnp.broadcast_to(c_t[:, T - 1:T], carry_ref.shape)
    return c_t


def _store_k_operand(k, c_parts, place_ref, kp_ref):
    hi, mid, lo = c_parts
    T = k.shape[0]
    cp_t = jnp.concatenate(
        [-hi, -mid, -lo, jnp.ones((8, T), F32), jnp.zeros((LANES - 3 * FOX_HEADS - 8, T), F32)], axis=0)
    cp = cp_t.T.astype(BF16)
    aug = _dot(cp, place_ref[...])
    lane = lax.broadcasted_iota(jnp.int32, (T, AUG), 1)
    for p in range(FOX_HEADS // 2):
        slab = k[:, p * AUG:(p + 1) * AUG]
        h0, h1 = 2 * p, 2 * p + 1
        kp_ref[0, h0] = jnp.where(lane < FOX_DH, slab, aug[:, h0 * AUG:(h0 + 1) * AUG]).astype(BF16)
        kp_ref[0, h1] = jnp.where(lane < FOX_DH, pltpu.roll(slab, FOX_DH, 1),
                                  aug[:, h1 * AUG:(h1 + 1) * AUG]).astype(BF16)


def _store_v_operand(v_t, vp_ref):
    T = v_t.shape[1]
    sub = lax.broadcasted_iota(jnp.int32, (FOX_DH, T), 0)
    ones_row = jnp.where(sub == 0, 1.0, 0.0).astype(BF16)
    for h in range(FOX_HEADS):
        vp_ref[0, h, 0, 0:FOX_DH, :] = v_t[h * FOX_DH:(h + 1) * FOX_DH].astype(BF16)
        vp_ref[0, h, 0, FOX_DH:AUG, :] = ones_row


def _store_q_operand(q_t, c_parts, qp_ref):
    hi, mid, lo = c_parts
    T = q_t.shape[1]
    sub = lax.broadcasted_iota(jnp.int32, (FOX_DH, T), 0)
    for h in range(FOX_HEADS):
        aug = jnp.where(sub == 0, hi[h:h + 1],
              jnp.where(sub == 1, mid[h:h + 1],
              jnp.where(sub == 2, lo[h:h + 1],
              jnp.where(sub < 6, 1.0, 0.0))))
        qp_ref[0, h, 0, 0:FOX_DH, :] = q_t[h * FOX_DH:(h + 1) * FOX_DH].astype(BF16)
        qp_ref[0, h, 0, FOX_DH:AUG, :] = aug.astype(BF16)


def _placement_matrix():
    p = np.zeros((LANES, FOX_HEADS * AUG), np.float32)
    for h in range(FOX_HEADS):
        for part in range(3):
            p[part * FOX_HEADS + h, h * AUG + _COL_KC[part]] = 1.0
            p[3 * FOX_HEADS, h * AUG + _COL_QC[part]] = 1.0
    return jnp.asarray(p, BF16)


def _fox_proj_kernel(x_ref, carry0_ref, wk_ref, wv_ref, wvt_ref, wqt_ref, wft_ref, bf_ref, place_ref,
                     k_ref, v_ref, lf_ref, kp_ref, vp_ref, qp_ref, carry_ref):
    @pl.when(pl.program_id(1) == 0)
    def _():
        carry_ref[...] = carry0_ref[0]

    xb = x_ref[0].astype(BF16)
    logf_t = jax.nn.log_sigmoid(_dot_nt(wft_ref[...], xb) + bf_ref[...])
    lf_ref[0] = logf_t
    c_parts = _split3(_advance_cumsum(logf_t, carry_ref) * LOG2E)
    _store_v_operand(_dot_nt(wvt_ref[...], xb), vp_ref)
    v_ref[0] = _dot(xb, wv_ref[...])
    _store_q_operand(_dot_nt(wqt_ref[...], xb), c_parts, qp_ref)
    k = _dot(xb, wk_ref[...])
    k_ref[0] = k
    _store_k_operand(k, c_parts, place_ref, kp_ref)


def _fox_projection(x, carry0, w_k, w_v, w_vt, w_qt, w_ft, b_f, tile):
    B, L, D = x.shape
    T = min(tile, L)
    assert L % T == 0 and T % LANES == 0
    nt = L // T
    H = FOX_HEADS
    return pl.pallas_call(
        _fox_proj_kernel,
        grid=(B, nt),
        in_specs=[
            pl.BlockSpec((1, T, D), lambda b, t: (b, t, 0)),
            pl.BlockSpec((1, H, LANES), lambda b, t: (b, 0, 0)),
            _const_spec(w_k.shape), _const_spec(w_v.shape), _const_spec(w_vt.shape),
            _const_spec(w_qt.shape), _const_spec(w_ft.shape), _const_spec((H, 1)),
            _const_spec((LANES, H * AUG)),
        ],
        out_specs=[
            pl.BlockSpec((1, T, HB), lambda b, t: (b, t, 0)),
            pl.BlockSpec((1, T, HB), lambda b, t: (b, t, 0)),
            pl.BlockSpec((1, H, T), lambda b, t: (b, 0, t)),
            pl.BlockSpec((1, H, T, AUG), lambda b, t: (b, 0, t, 0)),
            pl.BlockSpec((1, H, 1, AUG, T), lambda b, t: (b, 0, t, 0, 0)),
            pl.BlockSpec((1, H, 1, AUG, T), lambda b, t: (b, 0, t, 0, 0)),
        ],
        out_shape=[
            jax.ShapeDtypeStruct((B, L, HB), F32),
            jax.ShapeDtypeStruct((B, L, HB), F32),
            jax.ShapeDtypeStruct((B, H, L), F32),
            jax.ShapeDtypeStruct((B, H, L, AUG), BF16),
            jax.ShapeDtypeStruct((B, H, nt, AUG, T), BF16),
            jax.ShapeDtypeStruct((B, H, nt, AUG, T), BF16),
        ],
        scratch_shapes=[pltpu.VMEM((H, LANES), F32)],
        compiler_params=_params(("arbitrary", "arbitrary")),
        name="fox_projection",
    )(x, carry0, w_k, w_v, w_vt, w_qt, w_ft, b_f.reshape(H, 1).astype(F32), _placement_matrix())


def _cache_kernel(k_ref, v_ref, lf_ref, eye_ref, place_ref, kp_ref, vp_ref, carry_ref):
    @pl.when(pl.program_id(1) == 0)
    def _():
        carry_ref[...] = jnp.zeros_like(carry_ref)

    c_parts = _split3(_advance_cumsum(lf_ref[0], carry_ref.at[0]) * LOG2E)
    _store_k_operand(k_ref[0], c_parts, place_ref, kp_ref)
    _store_v_operand(_dot_nt(eye_ref[...], v_ref[0].astype(BF16)), vp_ref)


def _cache_operands(cache_k, cache_v, cache_logf_t, tile):
    B, P, _ = cache_k.shape
    T = min(tile, P)
    assert P % T == 0 and T % LANES == 0
    nt = P // T
    H = FOX_HEADS
    return pl.pallas_call(
        _cache_kernel,
        grid=(B, nt),
        in_specs=[
            pl.BlockSpec((1, T, HB), lambda b, t: (b, t, 0)),
            pl.BlockSpec((1, T, HB), lambda b, t: (b, t, 0)),
            pl.BlockSpec((1, H, T), lambda b, t: (b, 0, t)),
            _const_spec((HB, HB)),
            _const_spec((LANES, H * AUG)),
        ],
        out_specs=[
            pl.BlockSpec((1, H, T, AUG), lambda b, t: (b, 0, t, 0)),
            pl.BlockSpec((1, H, 1, AUG, T), lambda b, t: (b, 0, t, 0, 0)),
            pl.BlockSpec((1, H, LANES), lambda b, t: (b, 0, 0)),
        ],
        out_shape=[
            jax.ShapeDtypeStruct((B, H, P, AUG), BF16),
            jax.ShapeDtypeStruct((B, H, nt, AUG, T), BF16),
            jax.ShapeDtypeStruct((B, H, LANES), F32),
        ],
        compiler_params=_params(("arbitrary", "arbitrary")),
        name="cache_operands",
    )(cache_k, cache_v, cache_logf_t, jnp.eye(HB, dtype=BF16), _placement_matrix())


def _finish_attention(acc):
    return (acc[0:FOX_DH] / acc[FOX_DH:FOX_DH + 1]).astype(BF16)


def _prompt_attn_kernel(ti_ref, tj_ref, q_ref, k_ref, v_ref, o_ref,
                        sa_ref, sb_ref, bma_ref, bmb_ref, m_ref, acc_ref,
                        *, n_masked, n_plain, unroll, bf16_exp):
    nq, _, T = q_ref.shape[2:]

    def k_block(j):
        return k_ref[0, 0, pl.ds(pl.multiple_of(j * T, T), T), :]

    def init(i, carry):
        m_ref[i] = jnp.full((1, T), NEG_INF, F32)
        acc_ref[i] = jnp.zeros((AUG, T), F32)
        return carry

    lax.fori_loop(0, nq + 1, init, 0)

    bufs = ((sa_ref, bma_ref), (sb_ref, bmb_ref))

    def produce(t, slot, masked=False):
        s_ref, bm_ref = bufs[slot]
        k_blk = k_block(tj_ref[t])
        qi = jnp.minimum(ti_ref[t], nq - 1)
        s = _dot(k_blk, q_ref[0, 0, qi])
        if masked:
            key = lax.broadcasted_iota(jnp.int32, (T, T), 0)
            qry = lax.broadcasted_iota(jnp.int32, (T, T), 1)
            s = jnp.where(key <= qry, s, NEG_INF)
        s_ref[...] = s
        bm_ref[...] = jnp.max(s, axis=0, keepdims=True)

    def consume(t, slot):
        s_ref, bm_ref = bufs[slot]
        i = ti_ref[t]
        j = tj_ref[t]
        m_prev = m_ref[i]
        m_new = jnp.maximum(m_prev, bm_ref[...])
        x = s_ref[...] - m_new
        p = jnp.exp2(x.astype(BF16)) if bf16_exp else jnp.exp2(x).astype(BF16)
        acc_ref[i] = jnp.exp2(m_prev - m_new) * acc_ref[i] + _dot(v_ref[0, 0, j], p)
        m_ref[i] = m_new

    produce(0, 0)

    def trip(u, carry, mask_last):
        t0 = unroll * u
        for k in range(unroll):
            produce(t0 + k + 1, (k + 1) % 2, masked=mask_last and k == unroll - 1)
            consume(t0 + k, k % 2)
        return carry

    lax.fori_loop(0, n_masked, functools.partial(trip, mask_last=True), 0)
    lax.fori_loop(n_masked, n_masked + n_plain, functools.partial(trip, mask_last=False), 0)

    def finish(i, carry):
        o_ref[0, 0, i] = _finish_attention(acc_ref[i])
        return carry

    lax.fori_loop(0, nq, finish, 0)


def _attention_tasks(nq, unroll):
    pad = (nq, 0)
    off = [(i, j) for i in range(nq) for j in range(i)]
    n_trips = max(nq + 1, -(-(len(off) + nq) // unroll))
    tasks = [None] * (n_trips * unroll + 1)
    for i in range(nq):
        tasks[unroll * (i + 1)] = (i, i)
    rest = iter(off)
    tasks = [t if t is not None else next(rest, pad) for t in tasks]
    ti = jnp.asarray([t[0] for t in tasks], jnp.int32)
    tj = jnp.asarray([t[1] for t in tasks], jnp.int32)
    return ti, tj, nq, n_trips - nq


def _prompt_attention(qp, kp, vp, b0, nb, h0, nh, *, unroll, bf16_exp, name):
    _, _, nq, _, T = qp.shape
    L = nq * T
    assert unroll % 2 == 0
    ti, tj, n_masked, n_plain = _attention_tasks(nq, unroll)
    grid_spec = pltpu.PrefetchScalarGridSpec(
        num_scalar_prefetch=2,
        grid=(nb, nh),
        in_specs=[
            pl.BlockSpec((1, 1, nq, AUG, T), lambda b, h, ti, tj: (b0 + b, h0 + h, 0, 0, 0)),
            pl.BlockSpec((1, 1, L, AUG), lambda b, h, ti, tj: (b0 + b, h0 + h, 0, 0)),
            pl.BlockSpec((1, 1, nq, AUG, T), lambda b, h, ti, tj: (b0 + b, h0 + h, 0, 0, 0)),
        ],
        out_specs=pl.BlockSpec((1, 1, nq, FOX_DH, T), lambda b, h, ti, tj: (b, h, 0, 0, 0)),
        scratch_shapes=[
            pltpu.VMEM((T, T), F32), pltpu.VMEM((T, T), F32),
            pltpu.VMEM((1, T), F32), pltpu.VMEM((1, T), F32),
            pltpu.VMEM((nq + 1, 1, T), F32),
            pltpu.VMEM((nq + 1, AUG, T), F32),
        ],
    )
    kern = functools.partial(_prompt_attn_kernel, n_masked=n_masked, n_plain=n_plain,
                             unroll=unroll, bf16_exp=bf16_exp)
    return pl.pallas_call(
        kern,
        grid_spec=grid_spec,
        out_shape=jax.ShapeDtypeStruct((nb, nh, nq, FOX_DH, T), BF16),
        compiler_params=_params(("arbitrary", "arbitrary")),
        name=name,
    )(ti, tj, qp, kp, vp)


def _sample_attn_kernel(q_ref, kc_ref, vc_ref, kn_ref, vn_ref, o_ref):
    q_t = q_ref[0, 0, 0]
    T = q_t.shape[1]
    n_c, _, t_c = vc_ref.shape[2:]
    s_c = _dot(kc_ref[0, 0], q_t)
    s_n = _dot(kn_ref[0, 0], q_t)
    key = lax.broadcasted_iota(jnp.int32, (T, T), 0)
    qry = lax.broadcasted_iota(jnp.int32, (T, T), 1)
    s_n = jnp.where(key <= qry, s_n, NEG_INF)
    m = jnp.maximum(jnp.max(s_c, axis=0, keepdims=True), jnp.max(s_n, axis=0, keepdims=True))
    p_c = jnp.exp2(s_c - m).astype(BF16)
    acc = _dot(vn_ref[0, 0, 0], jnp.exp2(s_n - m).astype(BF16))
    for j in range(n_c):
        acc = acc + _dot(vc_ref[0, 0, j], p_c[j * t_c:(j + 1) * t_c])
    o_ref[0, 0, 0] = _finish_attention(acc)


def _sample_attention(qp, kp_cache, vp_cache, kp_new, vp_new):
    B, H, _, _, T = qp.shape
    P = kp_cache.shape[2]
    n_c, _, t_c = vp_cache.shape[2:]
    return pl.pallas_call(
        _sample_attn_kernel,
        grid=(B, H),
        in_specs=[
            pl.BlockSpec((1, 1, 1, AUG, T), lambda b, h: (b, h, 0, 0, 0)),
            pl.BlockSpec((1, 1, P, AUG), lambda b, h: (b, h, 0, 0)),
            pl.BlockSpec((1, 1, n_c, AUG, t_c), lambda b, h: (b, h, 0, 0, 0)),
            pl.BlockSpec((1, 1, T, AUG), lambda b, h: (b, h, 0, 0)),
            pl.BlockSpec((1, 1, 1, AUG, T), lambda b, h: (b, h, 0, 0, 0)),
        ],
        out_specs=pl.BlockSpec((1, 1, 1, FOX_DH, T), lambda b, h: (b, h, 0, 0, 0)),
        out_shape=jax.ShapeDtypeStruct((B, H, 1, FOX_DH, T), BF16),
        compiler_params=_params(("arbitrary", "arbitrary")),
        name="sample_attention",
    )(qp, kp_cache, vp_cache, kp_new, vp_new)


ATTN_UNROLL = 16
RET_TILE = 512
FFN_TILE = 512
FOX_TILE = 512
CACHE_TILE = 512


def kernel(x_prompt, x_sample, cache_k, cache_v, cache_logf, state_ret, state_ffn_conv,
           w_in_a, ln_ret_g, ln_ret_b, w_out_a, w_kvf, b_f, w_q_b, w_out_b,
           ln_mix_g, ln_mix_b, w_up, conv_w, conv_b, w_down, ln_ffn_g, ln_ffn_b):
    Bp, Lp, _ = x_prompt.shape
    Bs, Ls, _ = x_sample.shape
    past = cache_k.shape[1]
    H = FOX_HEADS

    w_in = w_in_a[0].astype(BF16)
    w_oa = w_out_a[0].astype(BF16)
    w_k = w_kvf[:, :HB].astype(BF16)
    w_v = w_kvf[:, HB:2 * HB].astype(BF16)
    w_vt = w_v.T
    w_ft = w_kvf[:, 2 * HB:].T.astype(BF16)
    w_qt = (w_q_b[0] * (FOX_DH ** -0.5 * LOG2E)).T.astype(BF16)
    w_ob = w_out_b[0].astype(BF16)
    w_up_b = w_up.astype(BF16)
    w_dn_b = w_down.astype(BF16)

    def layer0(x, pos0, ret_state0, conv_state0, ret_tile, ffn_tile):
        x1, ret_new = _retention_layer(x, pos0, ret_state0, w_in, w_oa, ln_ret_g[0], ln_ret_b[0],
                                       ln_mix_g[0], ln_mix_b[0], ret_tile)
        x2, cs0 = _ffn_layer(x1, conv_state0, w_up_b[0], conv_w[0], conv_b[0], w_dn_b[0],
                             ln_ffn_g[0], ln_ffn_b[0], ffn_tile)
        return x2, ret_new, cs0

    def ffn1(x2, o_t, conv_state0, tile, n_valid=None):
        return _ffn_layer(x2, conv_state0, w_up_b[1], conv_w[1], conv_b[1], w_dn_b[1],
                          ln_ffn_g[1], ln_ffn_b[1], tile, n_valid=n_valid,
                          mix=(o_t, w_ob, ln_mix_g[1], ln_mix_b[1]))

    zeros_ret = jnp.zeros((Bp, RET_HEADS, RET_DK, RET_DV), F32)
    zeros_cs = jnp.zeros((Bp, CONV_W - 1, D_FF), F32)
    x2p, ret_p, cs0_p = layer0(x_prompt, 0, zeros_ret, zeros_cs, RET_TILE, FFN_TILE)
    kp_, vp_, lft_p, kop, vop, qop = _fox_projection(
        x2p, jnp.zeros((Bp, H, LANES), F32), w_k, w_v, w_vt, w_qt, w_ft, b_f, FOX_TILE)
    assert Bp == 2
    ot_p = jnp.concatenate([
        _prompt_attention(qop, kop, vop, b, 1, 0, H, unroll=ATTN_UNROLL, bf16_exp=bf16_exp,
                          name="prompt_attention_" + tag)
        for b, (tag, bf16_exp) in enumerate((("f32exp", False), ("bf16exp", True)))], axis=0)
    y_p, cs1_p = ffn1(x2p, ot_p, zeros_cs, FFN_TILE)

    x2s, ret_s, cs0_s = layer0(x_sample, past, state_ret[0], state_ffn_conv[0], Ls, Ls)
    kc, vc, carry = _cache_operands(cache_k.reshape(Bs, past, HB), cache_v.reshape(Bs, past, HB),
                                    jnp.transpose(cache_logf.astype(F32), (0, 2, 1)), CACHE_TILE)
    Tpad = max(LANES, Ls)
    x2s_pad = jnp.pad(x2s, ((0, 0), (0, Tpad - Ls), (0, 0)))
    ks_, vs_, lft_s, kos, vos, qos = _fox_projection(x2s_pad, carry, w_k, w_v, w_vt, w_qt, w_ft, b_f, Tpad)
    ot_s = _sample_attention(qos, kc, vc, kos, vos)
    y_s_pad, cs1_s = ffn1(x2s_pad, ot_s, state_ffn_conv[1], Tpad, n_valid=Ls)

    def heads(a, L):
        return a[:, :L].reshape(a.shape[0], L, H, FOX_DH)

    return (y_p, y_s_pad[:, :Ls],
            ret_p[None], heads(kp_, Lp), heads(vp_, Lp), jnp.transpose(lft_p, (0, 2, 1)),
            jnp.stack([cs0_p, cs1_p]),
            ret_s[None], heads(ks_, Ls), heads(vs_, Ls), jnp.transpose(lft_s[:, :, :Ls], (0, 2, 1)),
            jnp.stack([cs0_s, cs1_s]))
```

```python
import functools
import math

import numpy as np
import jax
import jax.numpy as jnp
from jax import lax
from jax.experimental import pallas as pl
from jax.experimental.pallas import tpu as pltpu

F32 = jnp.float32
BF16 = jnp.bfloat16

D_MODEL = 1024
RET_DK = 256
RET_HEADS = D_MODEL // RET_DK
RET_DV = 2 * RET_DK
HK = RET_HEADS * RET_DK
HV = RET_HEADS * RET_DV
FOX_DH = 64
FOX_HEADS = D_MODEL // FOX_DH
HB = FOX_HEADS * FOX_DH
D_FF = 2816
CONV_W = 3
ROPE_BASE = 10000.0
LN_EPS = 1e-5
DEPTH = 2
ALPHA = (2 * DEPTH) ** 0.25
NEG_INF = -1e30
LOG2E = math.log2(math.e)

LANES = 128
AUG = 2 * FOX_DH
FF_CHUNK = 256
VMEM_LIMIT = 56 * 1024 * 1024

_COL_QC = (FOX_DH, FOX_DH + 1, FOX_DH + 2)
_COL_KC = (FOX_DH + 3, FOX_DH + 4, FOX_DH + 5)


def _dot(a, b):
    return jnp.dot(a, b, preferred_element_type=F32)


def _dot_nt(a, b):
    return lax.dot_general(a, b, (((1,), (1,)), ((), ())), preferred_element_type=F32)


def _dot_tn(a, b):
    return lax.dot_general(a, b, (((0,), (0,)), ((), ())), preferred_element_type=F32)


def _layernorm(z, g, b):
    mu = jnp.mean(z, axis=-1, keepdims=True)
    zc = z - mu
    var = jnp.mean(zc * zc, axis=-1, keepdims=True)
    return zc * lax.rsqrt(var + LN_EPS) * g + b


def _row_halves(n):
    if n >= 256 and n % 32 == 0:
        return [slice(0, n // 2), slice(n // 2, n)]
    return [slice(0, n)]


def _const_spec(shape):
    zeros = (0,) * len(shape)
    return pl.BlockSpec(shape, lambda *_: zeros, pipeline_mode=pl.Buffered(1))


def _params(sem, flags=None):
    return pltpu.CompilerParams(dimension_semantics=sem, vmem_limit_bytes=VMEM_LIMIT, flags=flags)


def _retention_kernel(x_ref, cos_ref, sin_ref, dmask_ref, qdec_ref, kdec_ref, s0_ref,
                      win_ref, wout_ref, lrg_ref, lrb_ref, lmg_ref, lmb_ref,
                      y_ref, s_ref, *, state_decay):
    @pl.when(pl.program_id(1) == 0)
    def _():
        s_ref[...] = s0_ref[...]

    x = x_ref[0]
    xb = x.astype(BF16)
    cos = cos_ref[...]
    sin = sin_ref[...]
    half = RET_DK // 2

    def rope(u):
        u1, u2 = u[:, :half], u[:, half:]
        return jnp.concatenate([u1 * cos - u2 * sin, u2 * cos + u1 * sin], axis=1)

    mix = None
    for h in range(RET_HEADS):
        q = rope(_dot(xb, win_ref[:, h * RET_DK:(h + 1) * RET_DK]))
        k = rope(_dot(xb, win_ref[:, HK + h * RET_DK:HK + (h + 1) * RET_DK])) * (RET_DK ** -0.5)
        v = _dot(xb, win_ref[:, 2 * HK + h * RET_DV:2 * HK + (h + 1) * RET_DV])
        g = _dot(xb, win_ref[:, 2 * HK + HV + h * RET_DV:2 * HK + HV + (h + 1) * RET_DV])
        vb = v.astype(BF16)
        scores = _dot_nt(q.astype(BF16), k.astype(BF16)) * dmask_ref[h]
        state = s_ref[0, h]
        o = _dot(scores.astype(BF16), vb) + _dot((q * qdec_ref[h]).astype(BF16), state.astype(BF16))
        s_ref[0, h] = state_decay[h] * state + _dot_tn((k * kdec_ref[h]).astype(BF16), vb)
        sl = slice(h * RET_DV, (h + 1) * RET_DV)
        o = _layernorm(o, lrg_ref[:, sl], lrb_ref[:, sl])
        gated = (jax.nn.silu(g) * o).astype(BF16)
        if h < RET_HEADS - 1:
            part = _dot(gated, wout_ref[sl, :])
            mix = part if mix is None else mix + part
        else:
            for rows in _row_halves(x.shape[0]):
                z = ALPHA * x[rows] + mix[rows] + _dot(gated[rows], wout_ref[sl, :])
                y_ref[0, rows, :] = _layernorm(z, lmg_ref[...], lmb_ref[...])


def _retention_layer(x, pos0, state0, w_in, w_out, ln_ret_g, ln_ret_b, ln_mix_g, ln_mix_b, tile):
    B, L, D = x.shape
    T = min(tile, L)
    assert L % T == 0
    nt = L // T
    pos = (pos0 + jnp.arange(L, dtype=jnp.int32)).astype(F32)
    half = RET_DK // 2
    inv = 1.0 / (ROPE_BASE ** (jnp.arange(half, dtype=F32) / half))
    ang = pos[:, None] * inv[None, :]
    cos, sin = jnp.cos(ang), jnp.sin(ang)
    lg = np.log1p(-np.exp2(-5.0 - np.arange(RET_HEADS, dtype=np.float64)))
    idx = np.arange(T, dtype=np.float64)
    diff = idx[:, None] - idx[None, :]
    dmask = np.where(diff >= 0, np.exp(lg[:, None, None] * np.maximum(diff, 0.0)), 0.0)
    qdec = np.exp((idx[None, :] + 1.0) * lg[:, None])[..., None]
    kdec = np.exp((T - 1.0 - idx)[None, :] * lg[:, None])[..., None]
    state_decay = tuple(float(v) for v in np.exp(T * lg))

    vec = lambda a: a.reshape(1, -1).astype(F32)
    kern = functools.partial(_retention_kernel, state_decay=state_decay)
    y, s_new = pl.pallas_call(
        kern,
        grid=(B, nt),
        in_specs=[
            pl.BlockSpec((1, T, D), lambda b, t: (b, t, 0)),
            pl.BlockSpec((T, half), lambda b, t: (t, 0)),
            pl.BlockSpec((T, half), lambda b, t: (t, 0)),
            _const_spec((RET_HEADS, T, T)),
            _const_spec((RET_HEADS, T, 1)),
            _const_spec((RET_HEADS, T, 1)),
            pl.BlockSpec((1, RET_HEADS, RET_DK, RET_DV), lambda b, t: (b, 0, 0, 0)),
            _const_spec(w_in.shape),
            _const_spec(w_out.shape),
            _const_spec((1, HV)), _const_spec((1, HV)),
            _const_spec((1, D)), _const_spec((1, D)),
        ],
        out_specs=[
            pl.BlockSpec((1, T, D), lambda b, t: (b, t, 0)),
            pl.BlockSpec((1, RET_HEADS, RET_DK, RET_DV), lambda b, t: (b, 0, 0, 0)),
        ],
        out_shape=[
            jax.ShapeDtypeStruct((B, L, D), F32),
            jax.ShapeDtypeStruct((B, RET_HEADS, RET_DK, RET_DV), F32),
        ],
        compiler_params=_params(("arbitrary", "arbitrary")),
        name="retention_layer",
    )(x, cos, sin, jnp.asarray(dmask, F32), jnp.asarray(qdec, F32), jnp.asarray(kdec, F32),
      state0, w_in, w_out, vec(ln_ret_g), vec(ln_ret_b), vec(ln_mix_g), vec(ln_mix_b))
    return y, s_new


def _ffn_kernel(*refs, n_valid, with_mix):
    if with_mix:
        (x_ref, ot_ref, wo_ref, lmg_ref, lmb_ref, cs0_ref, wup_ref, cw_ref, cb_ref, wdn_ref,
         lfg_ref, lfb_ref, y_ref, cs_ref, hid_ref) = refs
    else:
        (x_ref, cs0_ref, wup_ref, cw_ref, cb_ref, wdn_ref,
         lfg_ref, lfb_ref, y_ref, cs_ref, hid_ref) = refs

    @pl.when(pl.program_id(1) == 0)
    def _():
        cs_ref[...] = cs0_ref[...]

    x = x_ref[0]
    if with_mix:
        o_t = ot_ref[0, :, 0].reshape(HB, x.shape[0])
        x = jnp.concatenate(
            [_layernorm(ALPHA * x[rows] + _dot_tn(o_t[:, rows], wo_ref[...]), lmg_ref[...], lmb_ref[...])
             for rows in _row_halves(x.shape[0])], axis=0)
    xb = x.astype(BF16)
    T = x.shape[0]
    row = lax.broadcasted_iota(jnp.int32, (T, FF_CHUNK), 0)
    for c in range(D_FF // FF_CHUNK):
        sl = slice(c * FF_CHUNK, (c + 1) * FF_CHUNK)
        val = _dot(xb, wup_ref[:, sl])
        a = _dot(xb, wup_ref[:, D_FF + c * FF_CHUNK:D_FF + (c + 1) * FF_CHUNK])
        prev2 = cs_ref[0, 0:1, sl]
        prev1 = cs_ref[0, 1:2, sl]
        a1 = jnp.where(row == 0, prev1, pltpu.roll(a, 1, 0))
        a2 = jnp.where(row == 0, prev2, jnp.where(row == 1, prev1, pltpu.roll(a, 2, 0)))
        conv = cb_ref[:, sl] + cw_ref[0:1, sl] * a2 + cw_ref[1:2, sl] * a1 + cw_ref[2:3, sl] * a
        gelu = 0.5 * conv * (1.0 + lax.erf(conv * (2.0 ** -0.5)))
        hid_ref[:, sl] = (gelu * val).astype(BF16)
        cs_ref[0, :, sl] = a[n_valid - 2:n_valid, :]
    for rows in _row_halves(T):
        f = _dot(hid_ref[rows, :], wdn_ref[...])
        y_ref[0, rows, :] = _layernorm(ALPHA * x[rows] + f, lfg_ref[...], lfb_ref[...])


def _ffn_layer(x, conv_state0, w_up, conv_w, conv_b, w_down, ln_g, ln_b, tile, n_valid=None,
               mix=None):
    B, L, D = x.shape
    T = min(tile, L)
    assert L % T == 0
    nt = L // T
    if n_valid is None:
        n_valid = T
    else:
        assert nt == 1
    vec = lambda a: a.reshape(1, -1).astype(F32)
    row_spec = pl.BlockSpec((1, T, D), lambda b, t: (b, t, 0))
    cs_spec = pl.BlockSpec((1, CONV_W - 1, D_FF), lambda b, t: (b, 0, 0))
    args, specs = [x], [row_spec]
    if mix is not None:
        o_t, w_o, lmg, lmb = mix
        args += [o_t, w_o, vec(lmg), vec(lmb)]
        per_blk = o_t.shape[4] // T
        assert o_t.shape[4] % T == 0 and o_t.shape[2] * per_blk == nt
        specs += [pl.BlockSpec((1, FOX_HEADS, 1, FOX_DH, T),
                               lambda b, t: (b, 0, t // per_blk, 0, t % per_blk)),
                  _const_spec(w_o.shape),
                  _const_spec((1, D)), _const_spec((1, D))]
    args += [conv_state0, w_up, conv_w, vec(conv_b), w_down, vec(ln_g), vec(ln_b)]
    specs += [cs_spec, _const_spec(w_up.shape), _const_spec(conv_w.shape), _const_spec((1, D_FF)),
              _const_spec(w_down.shape), _const_spec((1, D)), _const_spec((1, D))]
    kern = functools.partial(_ffn_kernel, n_valid=n_valid, with_mix=mix is not None)
    y, cs = pl.pallas_call(
        kern,
        grid=(B, nt),
        in_specs=specs,
        out_specs=[row_spec, cs_spec],
        out_shape=[jax.ShapeDtypeStruct((B, L, D), F32),
                   jax.ShapeDtypeStruct((B, CONV_W - 1, D_FF), F32)],
        scratch_shapes=[pltpu.VMEM((T, D_FF), BF16)],
        compiler_params=_params(("arbitrary", "arbitrary")),
        name="ffn_mix_layer" if mix is not None else "ffn_layer",
    )(*args)
    return y, cs


def _cumsum_lanes(x):
    n = x.shape[1]
    lane = lax.broadcasted_iota(jnp.int32, x.shape, 1)
    s = 1
    while s < n:
        x = x + jnp.where(lane >= s, pltpu.roll(x, s, 1), 0.0)
        s *= 2
    return x


def _split3(c):
    hi = c.astype(BF16).astype(F32)
    r = c - hi
    mid = r.astype(BF16).astype(F32)
    lo = (r - mid).astype(BF16).astype(F32)
    return hi, mid, lo


def _advance_cumsum(logf_t, carry_ref):
    c_t = carry_ref[:, 0:1] + _cumsum_lanes(logf_t)
    T = logf_t.shape[1]
    carry_ref[...] = jnp.broadcast_to(c_t[:, T - 1:T], carry_ref.shape)
    return c_t


def _store_k_operand(k, c_parts, place_ref, kp_ref):
    hi, mid, lo = c_parts
    T = k.shape[0]
    cp_t = jnp.concatenate(
        [-hi, -mid, -lo, jnp.ones((8, T), F32), jnp.zeros((LANES - 3 * FOX_HEADS - 8, T), F32)], axis=0)
    cp = cp_t.T.astype(BF16)
    aug = _dot(cp, place_ref[...])
    lane = lax.broadcasted_iota(jnp.int32, (T, AUG), 1)
    for p in range(FOX_HEADS // 2):
        slab = k[:, p * AUG:(p + 1) * AUG]
        h0, h1 = 2 * p, 2 * p + 1
        kp_ref[0, h0] = jnp.where(lane < FOX_DH, slab, aug[:, h0 * AUG:(h0 + 1) * AUG]).astype(BF16)
        kp_ref[0, h1] = jnp.where(lane < FOX_DH, pltpu.roll(slab, FOX_DH, 1),
                                  aug[:, h1 * AUG:(h1 + 1) * AUG]).astype(BF16)


def _store_v_operand(v_t, vp_ref):
    T = v_t.shape[1]
    sub = lax.broadcasted_iota(jnp.int32, (FOX_DH, T), 0)
    ones_row = jnp.where(sub == 0, 1.0, 0.0).astype(BF16)
    for h in range(FOX_HEADS):
        vp_ref[0, h, 0, 0:FOX_DH, :] = v_t[h * FOX_DH:(h + 1) * FOX_DH].astype(BF16)
        vp_ref[0, h, 0, FOX_DH:AUG, :] = ones_row


def _store_q_operand(q_t, c_parts, qp_ref):
    hi, mid, lo = c_parts
    T = q_t.shape[1]
    sub = lax.broadcasted_iota(jnp.int32, (FOX_DH, T), 0)
    for h in range(FOX_HEADS):
        aug = jnp.where(sub == 0, hi[h:h + 1],
              jnp.where(sub == 1, mid[h:h + 1],
              jnp.where(sub == 2, lo[h:h + 1],
              jnp.where(sub < 6, 1.0, 0.0))))
        qp_ref[0, h, 0, 0:FOX_DH, :] = q_t[h * FOX_DH:(h + 1) * FOX_DH].astype(BF16)
        qp_ref[0, h, 0, FOX_DH:AUG, :] = aug.astype(BF16)


def _placement_matrix():
    p = np.zeros((LANES, FOX_HEADS * AUG), np.float32)
    for h in range(FOX_HEADS):
        for part in range(3):
            p[part * FOX_HEADS + h, h * AUG + _COL_KC[part]] = 1.0
            p[3 * FOX_HEADS, h * AUG + _COL_QC[part]] = 1.0
    return jnp.asarray(p, BF16)


def _fox_proj_kernel(x_ref, carry0_ref, wk_ref, wv_ref, wvt_ref, wqt_ref, wft_ref, bf_ref, place_ref,
                     k_ref, v_ref, lf_ref, kp_ref, vp_ref, qp_ref, carry_ref):
    @pl.when(pl.program_id(1) == 0)
    def _():
        carry_ref[...] = carry0_ref[0]

    xb = x_ref[0].astype(BF16)
    logf_t = jax.nn.log_sigmoid(_dot_nt(wft_ref[...], xb) + bf_ref[...])
    lf_ref[0] = logf_t
    c_parts = _split3(_advance_cumsum(logf_t, carry_ref) * LOG2E)
    _store_v_operand(_dot_nt(wvt_ref[...], xb), vp_ref)
    v_ref[0] = _dot(xb, wv_ref[...])
    _store_q_operand(_dot_nt(wqt_ref[...], xb), c_parts, qp_ref)
    k = _dot(xb, wk_ref[...])
    k_ref[0] = k
    _store_k_operand(k, c_parts, place_ref, kp_ref)


def _fox_projection(x, carry0, w_k, w_v, w_vt, w_qt, w_ft, b_f, tile):
    B, L, D = x.shape
    T = min(tile, L)
    assert L % T == 0 and T % LANES == 0
    nt = L // T
    H = FOX_HEADS
    return pl.pallas_call(
        _fox_proj_kernel,
        grid=(B, nt),
        in_specs=[
            pl.BlockSpec((1, T, D), lambda b, t: (b, t, 0)),
            pl.BlockSpec((1, H, LANES), lambda b, t: (b, 0, 0)),
            _const_spec(w_k.shape), _const_spec(w_v.shape), _const_spec(w_vt.shape),
            _const_spec(w_qt.shape), _const_spec(w_ft.shape), _const_spec((H, 1)),
            _const_spec((LANES, H * AUG)),
        ],
        out_specs=[
            pl.BlockSpec((1, T, HB), lambda b, t: (b, t, 0)),
            pl.BlockSpec((1, T, HB), lambda b, t: (b, t, 0)),
            pl.BlockSpec((1, H, T), lambda b, t: (b, 0, t)),
            pl.BlockSpec((1, H, T, AUG), lambda b, t: (b, 0, t, 0)),
            pl.BlockSpec((1, H, 1, AUG, T), lambda b, t: (b, 0, t, 0, 0)),
            pl.BlockSpec((1, H, 1, AUG, T), lambda b, t: (b, 0, t, 0, 0)),
        ],
        out_shape=[
            jax.ShapeDtypeStruct((B, L, HB), F32),
            jax.ShapeDtypeStruct((B, L, HB), F32),
            jax.ShapeDtypeStruct((B, H, L), F32),
            jax.ShapeDtypeStruct((B, H, L, AUG), BF16),
            jax.ShapeDtypeStruct((B, H, nt, AUG, T), BF16),
            jax.ShapeDtypeStruct((B, H, nt, AUG, T), BF16),
        ],
        scratch_shapes=[pltpu.VMEM((H, LANES), F32)],
        compiler_params=_params(("arbitrary", "arbitrary")),
        name="fox_projection",
    )(x, carry0, w_k, w_v, w_vt, w_qt, w_ft, b_f.reshape(H, 1).astype(F32), _placement_matrix())


def _cache_kernel(k_ref, v_ref, lf_ref, eye_ref, place_ref, kp_ref, vp_ref, carry_ref):
    @pl.when(pl.program_id(1) == 0)
    def _():
        carry_ref[...] = jnp.zeros_like(carry_ref)

    c_parts = _split3(_advance_cumsum(lf_ref[0], carry_ref.at[0]) * LOG2E)
    _store_k_operand(k_ref[0], c_parts, place_ref, kp_ref)
    _store_v_operand(_dot_nt(eye_ref[...], v_ref[0].astype(BF16)), vp_ref)


def _cache_operands(cache_k, cache_v, cache_logf_t, tile):
    B, P, _ = cache_k.shape
    T = min(tile, P)
    assert P % T == 0 and T % LANES == 0
    nt = P // T
    H = FOX_HEADS
    return pl.pallas_call(
        _cache_kernel,
        grid=(B, nt),
        in_specs=[
            pl.BlockSpec((1, T, HB), lambda b, t: (b, t, 0)),
            pl.BlockSpec((1, T, HB), lambda b, t: (b, t, 0)),
            pl.BlockSpec((1, H, T), lambda b, t: (b, 0, t)),
            _const_spec((HB, HB)),
            _const_spec((LANES, H * AUG)),
        ],
        out_specs=[
            pl.BlockSpec((1, H, T, AUG), lambda b, t: (b, 0, t, 0)),
            pl.BlockSpec((1, H, 1, AUG, T), lambda b, t: (b, 0, t, 0, 0)),
            pl.BlockSpec((1, H, LANES), lambda b, t: (b, 0, 0)),
        ],
        out_shape=[
            jax.ShapeDtypeStruct((B, H, P, AUG), BF16),
            jax.ShapeDtypeStruct((B, H, nt, AUG, T), BF16),
            jax.ShapeDtypeStruct((B, H, LANES), F32),
        ],
        compiler_params=_params(("arbitrary", "arbitrary")),
        name="cache_operands",
    )(cache_k, cache_v, cache_logf_t, jnp.eye(HB, dtype=BF16), _placement_matrix())


def _finish_attention(acc):
    return (acc[0:FOX_DH] / acc[FOX_DH:FOX_DH + 1]).astype(BF16)


def _prompt_attn_kernel(ti_ref, tj_ref, q_ref, k_ref, v_ref, o_ref,
                        sa_ref, sb_ref, bma_ref, bmb_ref, m_ref, acc_ref,
                        *, n_masked, n_plain, unroll):
    nq, _, T = q_ref.shape[2:]

    def k_block(j):
        return k_ref[0, 0, pl.ds(pl.multiple_of(j * T, T), T), :]

    def init(i, carry):
        m_ref[i] = jnp.full((1, T), NEG_INF, F32)
        acc_ref[i] = jnp.zeros((AUG, T), F32)
        return carry

    lax.fori_loop(0, nq + 1, init, 0)

    bufs = ((sa_ref, bma_ref), (sb_ref, bmb_ref))

    def produce(t, slot, masked=False):
        s_ref, bm_ref = bufs[slot]
        k_blk = k_block(tj_ref[t])
        qi = jnp.minimum(ti_ref[t], nq - 1)
        s = _dot(k_blk, q_ref[0, 0, qi])
        if masked:
            key = lax.broadcasted_iota(jnp.int32, (T, T), 0)
            qry = lax.broadcasted_iota(jnp.int32, (T, T), 1)
            s = jnp.where(key <= qry, s, NEG_INF)
        s_ref[...] = s
        bm_ref[...] = jnp.max(s, axis=0, keepdims=True)

    def consume(t, slot):
        s_ref, bm_ref = bufs[slot]
        i = ti_ref[t]
        j = tj_ref[t]
        m_prev = m_ref[i]
        m_new = jnp.maximum(m_prev, bm_ref[...])
        p = jnp.exp2(s_ref[...] - m_new).astype(BF16)
        acc_ref[i] = jnp.exp2(m_prev - m_new) * acc_ref[i] + _dot(v_ref[0, 0, j], p)
        m_ref[i] = m_new

    produce(0, 0)

    def trip(u, carry, mask_last):
        t0 = unroll * u
        for k in range(unroll):
            produce(t0 + k + 1, (k + 1) % 2, masked=mask_last and k == unroll - 1)
            consume(t0 + k, k % 2)
        return carry

    lax.fori_loop(0, n_masked, functools.partial(trip, mask_last=True), 0)
    lax.fori_loop(n_masked, n_masked + n_plain, functools.partial(trip, mask_last=False), 0)

    def finish(i, carry):
        o_ref[0, 0, i] = _finish_attention(acc_ref[i])
        return carry

    lax.fori_loop(0, nq, finish, 0)


def _attention_tasks(nq, unroll):
    pad = (nq, 0)
    off = [(i, j) for i in range(nq) for j in range(i)]
    n_trips = max(nq + 1, -(-(len(off) + nq) // unroll))
    tasks = [None] * (n_trips * unroll + 1)
    for i in range(nq):
        tasks[unroll * (i + 1)] = (i, i)
    rest = iter(off)
    tasks = [t if t is not None else next(rest, pad) for t in tasks]
    ti = jnp.asarray([t[0] for t in tasks], jnp.int32)
    tj = jnp.asarray([t[1] for t in tasks], jnp.int32)
    return ti, tj, nq, n_trips - nq


def _prompt_attention(qp, kp, vp, *, unroll):
    nb, nh, nq, _, T = qp.shape
    L = nq * T
    assert unroll % 2 == 0
    ti, tj, n_masked, n_plain = _attention_tasks(nq, unroll)
    grid_spec = pltpu.PrefetchScalarGridSpec(
        num_scalar_prefetch=2,
        grid=(nb, nh),
        in_specs=[
            pl.BlockSpec((1, 1, nq, AUG, T), lambda b, h, ti, tj: (b, h, 0, 0, 0)),
            pl.BlockSpec((1, 1, L, AUG), lambda b, h, ti, tj: (b, h, 0, 0)),
            pl.BlockSpec((1, 1, nq, AUG, T), lambda b, h, ti, tj: (b, h, 0, 0, 0)),
        ],
        out_specs=pl.BlockSpec((1, 1, nq, FOX_DH, T), lambda b, h, ti, tj: (b, h, 0, 0, 0)),
        scratch_shapes=[
            pltpu.VMEM((T, T), F32), pltpu.VMEM((T, T), F32),
            pltpu.VMEM((1, T), F32), pltpu.VMEM((1, T), F32),
            pltpu.VMEM((nq + 1, 1, T), F32),
            pltpu.VMEM((nq + 1, AUG, T), F32),
        ],
    )
    kern = functools.partial(_prompt_attn_kernel, n_masked=n_masked, n_plain=n_plain,
                             unroll=unroll)
    return pl.pallas_call(
        kern,
        grid_spec=grid_spec,
        out_shape=jax.ShapeDtypeStruct((nb, nh, nq, FOX_DH, T), BF16),
        compiler_params=_params(("arbitrary", "arbitrary")),
        name="prompt_attention",
    )(ti, tj, qp, kp, vp)


def _sample_attn_kernel(q_ref, kc_ref, vc_ref, kn_ref, vn_ref, o_ref):
    q_t = q_ref[0, 0, 0]
    T = q_t.shape[1]
    n_c, _, t_c = vc_ref.shape[2:]
    s_c = _dot(kc_ref[0, 0], q_t)
    s_n = _dot(kn_ref[0, 0], q_t)
    key = lax.broadcasted_iota(jnp.int32, (T, T), 0)
    qry = lax.broadcasted_iota(jnp.int32, (T, T), 1)
    s_n = jnp.where(key <= qry, s_n, NEG_INF)
    m = jnp.maximum(jnp.max(s_c, axis=0, keepdims=True), jnp.max(s_n, axis=0, keepdims=True))
    p_c = jnp.exp2(s_c - m).astype(BF16)
    acc = _dot(vn_ref[0, 0, 0], jnp.exp2(s_n - m).astype(BF16))
    for j in range(n_c):
        acc = acc + _dot(vc_ref[0, 0, j], p_c[j * t_c:(j + 1) * t_c])
    o_ref[0, 0, 0] = _finish_attention(acc)


def _sample_attention(qp, kp_cache, vp_cache, kp_new, vp_new):
    B, H, _, _, T = qp.shape
    P = kp_cache.shape[2]
    n_c, _, t_c = vp_cache.shape[2:]
    return pl.pallas_call(
        _sample_attn_kernel,
        grid=(B, H),
        in_specs=[
            pl.BlockSpec((1, 1, 1, AUG, T), lambda b, h: (b, h, 0, 0, 0)),
            pl.BlockSpec((1, 1, P, AUG), lambda b, h: (b, h, 0, 0)),
            pl.BlockSpec((1, 1, n_c, AUG, t_c), lambda b, h: (b, h, 0, 0, 0)),
            pl.BlockSpec((1, 1, T, AUG), lambda b, h: (b, h, 0, 0)),
            pl.BlockSpec((1, 1, 1, AUG, T), lambda b, h: (b, h, 0, 0, 0)),
        ],
        out_specs=pl.BlockSpec((1, 1, 1, FOX_DH, T), lambda b, h: (b, h, 0, 0, 0)),
        out_shape=jax.ShapeDtypeStruct((B, H, 1, FOX_DH, T), BF16),
        compiler_params=_params(("arbitrary", "arbitrary")),
        name="sample_attention",
    )(qp, kp_cache, vp_cache, kp_new, vp_new)


ATTN_UNROLL = 16
RET_TILE = 512
FFN_TILE = 512
FOX_TILE = 512
CACHE_TILE = 512


def kernel(x_prompt, x_sample, cache_k, cache_v, cache_logf, state_ret, state_ffn_conv,
           w_in_a, ln_ret_g, ln_ret_b, w_out_a, w_kvf, b_f, w_q_b, w_out_b,
           ln_mix_g, ln_mix_b, w_up, conv_w, conv_b, w_down, ln_ffn_g, ln_ffn_b):
    Bp, Lp, _ = x_prompt.shape
    Bs, Ls, _ = x_sample.shape
    past = cache_k.shape[1]
    H = FOX_HEADS

    w_in = w_in_a[0].astype(BF16)
    w_oa = w_out_a[0].astype(BF16)
    w_k = w_kvf[:, :HB].astype(BF16)
    w_v = w_kvf[:, HB:2 * HB].astype(BF16)
    w_vt = w_v.T
    w_ft = w_kvf[:, 2 * HB:].T.astype(BF16)
    w_qt = (w_q_b[0] * (FOX_DH ** -0.5 * LOG2E)).T.astype(BF16)
    w_ob = w_out_b[0].astype(BF16)
    w_up_b = w_up.astype(BF16)
    w_dn_b = w_down.astype(BF16)

    def layer0(x, pos0, ret_state0, conv_state0, ret_tile, ffn_tile):
        x1, ret_new = _retention_layer(x, pos0, ret_state0, w_in, w_oa, ln_ret_g[0], ln_ret_b[0],
                                       ln_mix_g[0], ln_mix_b[0], ret_tile)
        x2, cs0 = _ffn_layer(x1, conv_state0, w_up_b[0], conv_w[0], conv_b[0], w_dn_b[0],
                             ln_ffn_g[0], ln_ffn_b[0], ffn_tile)
        return x2, ret_new, cs0

    def ffn1(x2, o_t, conv_state0, tile, n_valid=None):
        return _ffn_layer(x2, conv_state0, w_up_b[1], conv_w[1], conv_b[1], w_dn_b[1],
                          ln_ffn_g[1], ln_ffn_b[1], tile, n_valid=n_valid,
                          mix=(o_t, w_ob, ln_mix_g[1], ln_mix_b[1]))

    zeros_ret = jnp.zeros((Bp, RET_HEADS, RET_DK, RET_DV), F32)
    zeros_cs = jnp.zeros((Bp, CONV_W - 1, D_FF), F32)
    x2p, ret_p, cs0_p = layer0(x_prompt, 0, zeros_ret, zeros_cs, RET_TILE, FFN_TILE)
    kp_, vp_, lft_p, kop, vop, qop = _fox_projection(
        x2p, jnp.zeros((Bp, H, LANES), F32), w_k, w_v, w_vt, w_qt, w_ft, b_f, FOX_TILE)
    ot_p = _prompt_attention(qop, kop, vop, unroll=ATTN_UNROLL)
    y_p, cs1_p = ffn1(x2p, ot_p, zeros_cs, FFN_TILE)

    x2s, ret_s, cs0_s = layer0(x_sample, past, state_ret[0], state_ffn_conv[0], Ls, Ls)
    kc, vc, carry = _cache_operands(cache_k.reshape(Bs, past, HB), cache_v.reshape(Bs, past, HB),
                                    jnp.transpose(cache_logf.astype(F32), (0, 2, 1)), CACHE_TILE)
    Tpad = max(LANES, Ls)
    x2s_pad = jnp.pad(x2s, ((0, 0), (0, Tpad - Ls), (0, 0)))
    ks_, vs_, lft_s, kos, vos, qos = _fox_projection(x2s_pad, carry, w_k, w_v, w_vt, w_qt, w_ft, b_f, Tpad)
    ot_s = _sample_attention(qos, kc, vc, kos, vos)
    y_s_pad, cs1_s = ffn1(x2s_pad, ot_s, state_ffn_conv[1], Tpad, n_valid=Ls)

    def heads(a, L):
        return a[:, :L].reshape(a.shape[0], L, H, FOX_DH)

    return (y_p, y_s_pad[:, :Ls],
            ret_p[None], heads(kp_, Lp), heads(vp_, Lp), jnp.transpose(lft_p, (0, 2, 1)),
            jnp.stack([cs0_p, cs1_p]),
            ret_s[None], heads(ks_, Ls), heads(vs_, Ls), jnp.transpose(lft_s[:, :, :Ls], (0, 2, 1)),
            jnp.stack([cs0_s, cs1_s]))
```

```python
import functools
import math

import numpy as np
import jax
import jax.numpy as jnp
from jax import lax
from jax.experimental import pallas as pl
from jax.experimental.pallas import tpu as pltpu

F32 = jnp.float32
BF16 = jnp.bfloat16

D_MODEL = 1024
RET_DK = 256
RET_HEADS = D_MODEL // RET_DK
RET_DV = 2 * RET_DK
HK = RET_HEADS * RET_DK
HV = RET_HEADS * RET_DV
FOX_DH = 64
FOX_HEADS = D_MODEL // FOX_DH
HB = FOX_HEADS * FOX_DH
D_FF = 2816
CONV_W = 3
ROPE_BASE = 10000.0
LN_EPS = 1e-5
DEPTH = 2
ALPHA = (2 * DEPTH) ** 0.25
NEG_INF = -1e30
LOG2E = math.log2(math.e)

LANES = 128
AUG = 2 * FOX_DH
FF_CHUNK = 256
VMEM_LIMIT = 56 * 1024 * 1024

_COL_QC = (FOX_DH, FOX_DH + 1, FOX_DH + 2)
_COL_KC = (FOX_DH + 3, FOX_DH + 4, FOX_DH + 5)


def _dot(a, b):
    return jnp.dot(a, b, preferred_element_type=F32)


def _dot_nt(a, b):
    return lax.dot_general(a, b, (((1,), (1,)), ((), ())), preferred_element_type=F32)


def _dot_tn(a, b):
    return lax.dot_general(a, b, (((0,), (0,)), ((), ())), preferred_element_type=F32)


def _layernorm(z, g, b):
    mu = jnp.mean(z, axis=-1, keepdims=True)
    zc = z - mu
    var = jnp.mean(zc * zc, axis=-1, keepdims=True)
    return zc * lax.rsqrt(var + LN_EPS) * g + b


def _row_halves(n):
    if n >= 256 and n % 32 == 0:
        return [slice(0, n // 2), slice(n // 2, n)]
    return [slice(0, n)]


def _const_spec(shape):
    zeros = (0,) * len(shape)
    return pl.BlockSpec(shape, lambda *_: zeros, pipeline_mode=pl.Buffered(1))


def _params(sem, flags=None):
    return pltpu.CompilerParams(dimension_semantics=sem, vmem_limit_bytes=VMEM_LIMIT, flags=flags)


def _retention_kernel(x_ref, cos_ref, sin_ref, dmask_ref, qdec_ref, kdec_ref, s0_ref,
                      win_ref, wout_ref, lrg_ref, lrb_ref, lmg_ref, lmb_ref,
                      y_ref, s_ref, *, state_decay):
    @pl.when(pl.program_id(1) == 0)
    def _():
        s_ref[...] = s0_ref[...]

    x = x_ref[0]
    xb = x.astype(BF16)
    cos = cos_ref[...]
    sin = sin_ref[...]
    half = RET_DK // 2

    def rope(u):
        u1, u2 = u[:, :half], u[:, half:]
        return jnp.concatenate([u1 * cos - u2 * sin, u2 * cos + u1 * sin], axis=1)

    mix = None
    for h in range(RET_HEADS):
        q = rope(_dot(xb, win_ref[:, h * RET_DK:(h + 1) * RET_DK]))
        k = rope(_dot(xb, win_ref[:, HK + h * RET_DK:HK + (h + 1) * RET_DK])) * (RET_DK ** -0.5)
        v = _dot(xb, win_ref[:, 2 * HK + h * RET_DV:2 * HK + (h + 1) * RET_DV])
        g = _dot(xb, win_ref[:, 2 * HK + HV + h * RET_DV:2 * HK + HV + (h + 1) * RET_DV])
        vb = v.astype(BF16)
        scores = _dot_nt(q.astype(BF16), k.astype(BF16)) * dmask_ref[h]
        state = s_ref[0, h]
        o = _dot(scores.astype(BF16), vb) + _dot((q * qdec_ref[h]).astype(BF16), state.astype(BF16))
        s_ref[0, h] = state_decay[h] * state + _dot_tn((k * kdec_ref[h]).astype(BF16), vb)
        sl = slice(h * RET_DV, (h + 1) * RET_DV)
        o = _layernorm(o, lrg_ref[:, sl], lrb_ref[:, sl])
        gated = (jax.nn.silu(g) * o).astype(BF16)
        if h < RET_HEADS - 1:
            part = _dot(gated, wout_ref[sl, :])
            mix = part if mix is None else mix + part
        else:
            for rows in _row_halves(x.shape[0]):
                z = ALPHA * x[rows] + mix[rows] + _dot(gated[rows], wout_ref[sl, :])
                y_ref[0, rows, :] = _layernorm(z, lmg_ref[...], lmb_ref[...])


def _retention_layer(x, pos0, state0, w_in, w_out, ln_ret_g, ln_ret_b, ln_mix_g, ln_mix_b, tile):
    B, L, D = x.shape
    T = min(tile, L)
    assert L % T == 0
    nt = L // T
    pos = (pos0 + jnp.arange(L, dtype=jnp.int32)).astype(F32)
    half = RET_DK // 2
    inv = 1.0 / (ROPE_BASE ** (jnp.arange(half, dtype=F32) / half))
    ang = pos[:, None] * inv[None, :]
    cos, sin = jnp.cos(ang), jnp.sin(ang)
    lg = np.log1p(-np.exp2(-5.0 - np.arange(RET_HEADS, dtype=np.float64)))
    idx = np.arange(T, dtype=np.float64)
    diff = idx[:, None] - idx[None, :]
    dmask = np.where(diff >= 0, np.exp(lg[:, None, None] * np.maximum(diff, 0.0)), 0.0)
    qdec = np.exp((idx[None, :] + 1.0) * lg[:, None])[..., None]
    kdec = np.exp((T - 1.0 - idx)[None, :] * lg[:, None])[..., None]
    state_decay = tuple(float(v) for v in np.exp(T * lg))

    vec = lambda a: a.reshape(1, -1).astype(F32)
    kern = functools.partial(_retention_kernel, state_decay=state_decay)
    y, s_new = pl.pallas_call(
        kern,
        grid=(B, nt),
        in_specs=[
            pl.BlockSpec((1, T, D), lambda b, t: (b, t, 0)),
            pl.BlockSpec((T, half), lambda b, t: (t, 0)),
            pl.BlockSpec((T, half), lambda b, t: (t, 0)),
            _const_spec((RET_HEADS, T, T)),
            _const_spec((RET_HEADS, T, 1)),
            _const_spec((RET_HEADS, T, 1)),
            pl.BlockSpec((1, RET_HEADS, RET_DK, RET_DV), lambda b, t: (b, 0, 0, 0)),
            _const_spec(w_in.shape),
            _const_spec(w_out.shape),
            _const_spec((1, HV)), _const_spec((1, HV)),
            _const_spec((1, D)), _const_spec((1, D)),
        ],
        out_specs=[
            pl.BlockSpec((1, T, D), lambda b, t: (b, t, 0)),
            pl.BlockSpec((1, RET_HEADS, RET_DK, RET_DV), lambda b, t: (b, 0, 0, 0)),
        ],
        out_shape=[
            jax.ShapeDtypeStruct((B, L, D), F32),
            jax.ShapeDtypeStruct((B, RET_HEADS, RET_DK, RET_DV), F32),
        ],
        compiler_params=_params(("arbitrary", "arbitrary")),
        name="retention_layer",
    )(x, cos, sin, jnp.asarray(dmask, F32), jnp.asarray(qdec, F32), jnp.asarray(kdec, F32),
      state0, w_in, w_out, vec(ln_ret_g), vec(ln_ret_b), vec(ln_mix_g), vec(ln_mix_b))
    return y, s_new


def _ffn_kernel(*refs, n_valid, with_mix):
    if with_mix:
        (x_ref, ot_ref, wo_ref, lmg_ref, lmb_ref, cs0_ref, wup_ref, cw_ref, cb_ref, wdn_ref,
         lfg_ref, lfb_ref, y_ref, cs_ref, hid_ref) = refs
    else:
        (x_ref, cs0_ref, wup_ref, cw_ref, cb_ref, wdn_ref,
         lfg_ref, lfb_ref, y_ref, cs_ref, hid_ref) = refs

    @pl.when(pl.program_id(1) == 0)
    def _():
        cs_ref[...] = cs0_ref[...]

    x = x_ref[0]
    if with_mix:
        o_t = ot_ref[0, :, 0].reshape(HB, x.shape[0])
        x = jnp.concatenate(
            [_layernorm(ALPHA * x[rows] + _dot_tn(o_t[:, rows], wo_ref[...]), lmg_ref[...], lmb_ref[...])
             for rows in _row_halves(x.shape[0])], axis=0)
    xb = x.astype(BF16)
    T = x.shape[0]
    row = lax.broadcasted_iota(jnp.int32, (T, FF_CHUNK), 0)
    for c in range(D_FF // FF_CHUNK):
        sl = slice(c * FF_CHUNK, (c + 1) * FF_CHUNK)
        val = _dot(xb, wup_ref[:, sl])
        a = _dot(xb, wup_ref[:, D_FF + c * FF_CHUNK:D_FF + (c + 1) * FF_CHUNK])
        prev2 = cs_ref[0, 0:1, sl]
        prev1 = cs_ref[0, 1:2, sl]
        a1 = jnp.where(row == 0, prev1, pltpu.roll(a, 1, 0))
        a2 = jnp.where(row == 0, prev2, jnp.where(row == 1, prev1, pltpu.roll(a, 2, 0)))
        conv = cb_ref[:, sl] + cw_ref[0:1, sl] * a2 + cw_ref[1:2, sl] * a1 + cw_ref[2:3, sl] * a
        gelu = 0.5 * conv * (1.0 + lax.erf(conv * (2.0 ** -0.5)))
        hid_ref[:, sl] = (gelu * val).astype(BF16)
        cs_ref[0, :, sl] = a[n_valid - 2:n_valid, :]
    for rows in _row_halves(T):
        f = _dot(hid_ref[rows, :], wdn_ref[...])
        y_ref[0, rows, :] = _layernorm(ALPHA * x[rows] + f, lfg_ref[...], lfb_ref[...])


def _ffn_layer(x, conv_state0, w_up, conv_w, conv_b, w_down, ln_g, ln_b, tile, n_valid=None,
               mix=None):
    B, L, D = x.shape
    T = min(tile, L)
    assert L % T == 0
    nt = L // T
    if n_valid is None:
        n_valid = T
    else:
        assert nt == 1
    vec = lambda a: a.reshape(1, -1).astype(F32)
    row_spec = pl.BlockSpec((1, T, D), lambda b, t: (b, t, 0))
    cs_spec = pl.BlockSpec((1, CONV_W - 1, D_FF), lambda b, t: (b, 0, 0))
    args, specs = [x], [row_spec]
    if mix is not None:
        o_t, w_o, lmg, lmb = mix
        args += [o_t, w_o, vec(lmg), vec(lmb)]
        per_blk = o_t.shape[4] // T
        assert o_t.shape[4] % T == 0 and o_t.shape[2] * per_blk == nt
        specs += [pl.BlockSpec((1, FOX_HEADS, 1, FOX_DH, T),
                               lambda b, t: (b, 0, t // per_blk, 0, t % per_blk)),
                  _const_spec(w_o.shape),
                  _const_spec((1, D)), _const_spec((1, D))]
    args += [conv_state0, w_up, conv_w, vec(conv_b), w_down, vec(ln_g), vec(ln_b)]
    specs += [cs_spec, _const_spec(w_up.shape), _const_spec(conv_w.shape), _const_spec((1, D_FF)),
              _const_spec(w_down.shape), _const_spec((1, D)), _const_spec((1, D))]
    kern = functools.partial(_ffn_kernel, n_valid=n_valid, with_mix=mix is not None)
    y, cs = pl.pallas_call(
        kern,
        grid=(B, nt),
        in_specs=specs,
        out_specs=[row_spec, cs_spec],
        out_shape=[jax.ShapeDtypeStruct((B, L, D), F32),
                   jax.ShapeDtypeStruct((B, CONV_W - 1, D_FF), F32)],
        scratch_shapes=[pltpu.VMEM((T, D_FF), BF16)],
        compiler_params=_params(("arbitrary", "arbitrary")),
        name="ffn_mix_layer" if mix is not None else "ffn_layer",
    )(*args)
    return y, cs


def _cumsum_lanes(x):
    n = x.shape[1]
    lane = lax.broadcasted_iota(jnp.int32, x.shape, 1)
    s = 1
    while s < n:
        x = x + jnp.where(lane >= s, pltpu.roll(x, s, 1), 0.0)
        s *= 2
    return x


def _split3(c):
    hi = c.astype(BF16).astype(F32)
    r = c - hi
    mid = r.astype(BF16).astype(F32)
    lo = (r - mid).astype(BF16).astype(F32)
    return hi, mid, lo


def _advance_cumsum(logf_t, carry_ref):
    c_t = carry_ref[:, 0:1] + _cumsum_lanes(logf_t)
    T = logf_t.shape[1]
    carry_ref[...] = jnp.broadcast_to(c_t[:, T - 1:T], carry_ref.shape)
    return c_t


def _store_k_operand(k, c_parts, place_ref, kp_ref):
    hi, mid, lo = c_parts
    T = k.shape[0]
    cp_t = jnp.concatenate(
        [-hi, -mid, -lo, jnp.ones((8, T), F32), jnp.zeros((LANES - 3 * FOX_HEADS - 8, T), F32)], axis=0)
    cp = cp_t.T.astype(BF16)
    aug = _dot(cp, place_ref[...])
    lane = lax.broadcasted_iota(jnp.int32, (T, AUG), 1)
    for p in range(FOX_HEADS // 2):
        slab = k[:, p * AUG:(p + 1) * AUG]
        h0, h1 = 2 * p, 2 * p + 1
        kp_ref[0, h0] = jnp.where(lane < FOX_DH, slab, aug[:, h0 * AUG:(h0 + 1) * AUG]).astype(BF16)
        kp_ref[0, h1] = jnp.where(lane < FOX_DH, pltpu.roll(slab, FOX_DH, 1),
                                  aug[:, h1 * AUG:(h1 + 1) * AUG]).astype(BF16)


def _store_v_operand(v_t, vp_ref):
    T = v_t.shape[1]
    sub = lax.broadcasted_iota(jnp.int32, (FOX_DH, T), 0)
    ones_row = jnp.where(sub == 0, 1.0, 0.0).astype(BF16)
    for h in range(FOX_HEADS):
        vp_ref[0, h, 0, 0:FOX_DH, :] = v_t[h * FOX_DH:(h + 1) * FOX_DH].astype(BF16)
        vp_ref[0, h, 0, FOX_DH:AUG, :] = ones_row


def _store_q_operand(q_t, c_parts, qp_ref):
    hi, mid, lo = c_parts
    T = q_t.shape[1]
    sub = lax.broadcasted_iota(jnp.int32, (FOX_DH, T), 0)
    for h in range(FOX_HEADS):
        aug = jnp.where(sub == 0, hi[h:h + 1],
              jnp.where(sub == 1, mid[h:h + 1],
              jnp.where(sub == 2, lo[h:h + 1],
              jnp.where(sub < 6, 1.0, 0.0))))
        qp_ref[0, h, 0, 0:FOX_DH, :] = q_t[h * FOX_DH:(h + 1) * FOX_DH].astype(BF16)
        qp_ref[0, h, 0, FOX_DH:AUG, :] = aug.astype(BF16)


def _placement_matrix():
    p = np.zeros((LANES, FOX_HEADS * AUG), np.float32)
    for h in range(FOX_HEADS):
        for part in range(3):
            p[part * FOX_HEADS + h, h * AUG + _COL_KC[part]] = 1.0
            p[3 * FOX_HEADS, h * AUG + _COL_QC[part]] = 1.0
    return jnp.asarray(p, BF16)


def _fox_proj_kernel(x_ref, carry0_ref, wk_ref, wvt_ref, wqt_ref, wft_ref, bf_ref, place_ref,
                     k_ref, v_ref, lf_ref, kp_ref, vp_ref, qp_ref, carry_ref):
    @pl.when(pl.program_id(1) == 0)
    def _():
        carry_ref[...] = carry0_ref[0]

    xb = x_ref[0].astype(BF16)
    logf_t = jax.nn.log_sigmoid(_dot_nt(wft_ref[...], xb) + bf_ref[...])
    lf_ref[0] = logf_t
    c_parts = _split3(_advance_cumsum(logf_t, carry_ref) * LOG2E)
    v_t = _dot_nt(wvt_ref[...], xb)
    _store_v_operand(v_t, vp_ref)
    v_ref[0] = v_t.T
    _store_q_operand(_dot_nt(wqt_ref[...], xb), c_parts, qp_ref)
    k = _dot(xb, wk_ref[...])
    k_ref[0] = k
    _store_k_operand(k, c_parts, place_ref, kp_ref)


def _fox_projection(x, carry0, w_k, w_vt, w_qt, w_ft, b_f, tile):
    B, L, D = x.shape
    T = min(tile, L)
    assert L % T == 0 and T % LANES == 0
    nt = L // T
    H = FOX_HEADS
    return pl.pallas_call(
        _fox_proj_kernel,
        grid=(B, nt),
        in_specs=[
            pl.BlockSpec((1, T, D), lambda b, t: (b, t, 0)),
            pl.BlockSpec((1, H, LANES), lambda b, t: (b, 0, 0)),
            _const_spec(w_k.shape), _const_spec(w_vt.shape),
            _const_spec(w_qt.shape), _const_spec(w_ft.shape), _const_spec((H, 1)),
            _const_spec((LANES, H * AUG)),
        ],
        out_specs=[
            pl.BlockSpec((1, T, HB), lambda b, t: (b, t, 0)),
            pl.BlockSpec((1, T, HB), lambda b, t: (b, t, 0)),
            pl.BlockSpec((1, H, T), lambda b, t: (b, 0, t)),
            pl.BlockSpec((1, H, T, AUG), lambda b, t: (b, 0, t, 0)),
            pl.BlockSpec((1, H, 1, AUG, T), lambda b, t: (b, 0, t, 0, 0)),
            pl.BlockSpec((1, H, 1, AUG, T), lambda b, t: (b, 0, t, 0, 0)),
        ],
        out_shape=[
            jax.ShapeDtypeStruct((B, L, HB), F32),
            jax.ShapeDtypeStruct((B, L, HB), F32),
            jax.ShapeDtypeStruct((B, H, L), F32),
            jax.ShapeDtypeStruct((B, H, L, AUG), BF16),
            jax.ShapeDtypeStruct((B, H, nt, AUG, T), BF16),
            jax.ShapeDtypeStruct((B, H, nt, AUG, T), BF16),
        ],
        scratch_shapes=[pltpu.VMEM((H, LANES), F32)],
        compiler_params=_params(("arbitrary", "arbitrary")),
        name="fox_projection",
    )(x, carry0, w_k, w_vt, w_qt, w_ft, b_f.reshape(H, 1).astype(F32), _placement_matrix())


def _cache_kernel(k_ref, v_ref, lf_ref, place_ref, kp_ref, vp_ref, carry_ref):
    @pl.when(pl.program_id(1) == 0)
    def _():
        carry_ref[...] = jnp.zeros_like(carry_ref)

    c_parts = _split3(_advance_cumsum(lf_ref[0], carry_ref.at[0]) * LOG2E)
    _store_k_operand(k_ref[0], c_parts, place_ref, kp_ref)
    _store_v_operand(v_ref[0].T, vp_ref)


def _cache_operands(cache_k, cache_v, cache_logf_t, tile):
    B, P, _ = cache_k.shape
    T = min(tile, P)
    assert P % T == 0 and T % LANES == 0
    nt = P // T
    H = FOX_HEADS
    return pl.pallas_call(
        _cache_kernel,
        grid=(B, nt),
        in_specs=[
            pl.BlockSpec((1, T, HB), lambda b, t: (b, t, 0)),
            pl.BlockSpec((1, T, HB), lambda b, t: (b, t, 0)),
            pl.BlockSpec((1, H, T), lambda b, t: (b, 0, t)),
            _const_spec((LANES, H * AUG)),
        ],
        out_specs=[
            pl.BlockSpec((1, H, T, AUG), lambda b, t: (b, 0, t, 0)),
            pl.BlockSpec((1, H, 1, AUG, T), lambda b, t: (b, 0, t, 0, 0)),
            pl.BlockSpec((1, H, LANES), lambda b, t: (b, 0, 0)),
        ],
        out_shape=[
            jax.ShapeDtypeStruct((B, H, P, AUG), BF16),
            jax.ShapeDtypeStruct((B, H, nt, AUG, T), BF16),
            jax.ShapeDtypeStruct((B, H, LANES), F32),
        ],
        compiler_params=_params(("arbitrary", "arbitrary")),
        name="cache_operands",
    )(cache_k, cache_v, cache_logf_t, _placement_matrix())


def _finish_attention(acc):
    return (acc[0:FOX_DH] / acc[FOX_DH:FOX_DH + 1]).astype(BF16)


def _prompt_attn_kernel(ti_ref, tj_ref, q_ref, k_ref, v_ref, o_ref,
                        sa_ref, sb_ref, bma_ref, bmb_ref, m_ref, acc_ref,
                        *, n_masked, n_plain, unroll):
    nq, _, T = q_ref.shape[2:]

    def k_block(j):
        return k_ref[0, 0, pl.ds(pl.multiple_of(j * T, T), T), :]

    def init(i, carry):
        m_ref[i] = jnp.full((1, T), NEG_INF, F32)
        acc_ref[i] = jnp.zeros((AUG, T), F32)
        return carry

    lax.fori_loop(0, nq + 1, init, 0)

    bufs = ((sa_ref, bma_ref), (sb_ref, bmb_ref))

    def produce(t, slot, masked=False):
        s_ref, bm_ref = bufs[slot]
        k_blk = k_block(tj_ref[t])
        qi = jnp.minimum(ti_ref[t], nq - 1)
        s = _dot(k_blk, q_ref[0, 0, qi])
        if masked:
            key = lax.broadcasted_iota(jnp.int32, (T, T), 0)
            qry = lax.broadcasted_iota(jnp.int32, (T, T), 1)
            s = jnp.where(key <= qry, s, NEG_INF)
        s_ref[...] = s
        bm_ref[...] = jnp.max(s, axis=0, keepdims=True)

    def consume(t, slot):
        s_ref, bm_ref = bufs[slot]
        i = ti_ref[t]
        j = tj_ref[t]
        m_prev = m_ref[i]
        m_new = jnp.maximum(m_prev, bm_ref[...])
        p = jnp.exp2(s_ref[...] - m_new).astype(BF16)
        acc_ref[i] = jnp.exp2(m_prev - m_new) * acc_ref[i] + _dot(v_ref[0, 0, j], p)
        m_ref[i] = m_new

    produce(0, 0)

    def trip(u, carry, mask_last):
        t0 = unroll * u
        for k in range(unroll):
            produce(t0 + k + 1, (k + 1) % 2, masked=mask_last and k == unroll - 1)
            consume(t0 + k, k % 2)
        return carry

    lax.fori_loop(0, n_masked, functools.partial(trip, mask_last=True), 0)
    lax.fori_loop(n_masked, n_masked + n_plain, functools.partial(trip, mask_last=False), 0)

    def finish(i, carry):
        o_ref[0, 0, i] = _finish_attention(acc_ref[i])
        return carry

    lax.fori_loop(0, nq, finish, 0)


def _attention_tasks(nq, unroll):
    pad = (nq, 0)
    off = [(i, j) for i in range(nq) for j in range(i)]
    n_trips = max(nq + 1, -(-(len(off) + nq) // unroll))
    tasks = [None] * (n_trips * unroll + 1)
    for i in range(nq):
        tasks[unroll * (i + 1)] = (i, i)
    rest = iter(off)
    tasks = [t if t is not None else next(rest, pad) for t in tasks]
    ti = jnp.asarray([t[0] for t in tasks], jnp.int32)
    tj = jnp.asarray([t[1] for t in tasks], jnp.int32)
    return ti, tj, nq, n_trips - nq


def _prompt_attention(qp, kp, vp, *, unroll):
    nb, nh, nq, _, T = qp.shape
    L = nq * T
    assert unroll % 2 == 0
    ti, tj, n_masked, n_plain = _attention_tasks(nq, unroll)
    grid_spec = pltpu.PrefetchScalarGridSpec(
        num_scalar_prefetch=2,
        grid=(nb, nh),
        in_specs=[
            pl.BlockSpec((1, 1, nq, AUG, T), lambda b, h, ti, tj: (b, h, 0, 0, 0)),
            pl.BlockSpec((1, 1, L, AUG), lambda b, h, ti, tj: (b, h, 0, 0)),
            pl.BlockSpec((1, 1, nq, AUG, T), lambda b, h, ti, tj: (b, h, 0, 0, 0)),
        ],
        out_specs=pl.BlockSpec((1, 1, nq, FOX_DH, T), lambda b, h, ti, tj: (b, h, 0, 0, 0)),
        scratch_shapes=[
            pltpu.VMEM((T, T), F32), pltpu.VMEM((T, T), F32),
            pltpu.VMEM((1, T), F32), pltpu.VMEM((1, T), F32),
            pltpu.VMEM((nq + 1, 1, T), F32),
            pltpu.VMEM((nq + 1, AUG, T), F32),
        ],
    )
    kern = functools.partial(_prompt_attn_kernel, n_masked=n_masked, n_plain=n_plain,
                             unroll=unroll)
    return pl.pallas_call(
        kern,
        grid_spec=grid_spec,
        out_shape=jax.ShapeDtypeStruct((nb, nh, nq, FOX_DH, T), BF16),
        compiler_params=_params(("arbitrary", "arbitrary")),
        name="prompt_attention",
    )(ti, tj, qp, kp, vp)


def _sample_attn_kernel(q_ref, kc_ref, vc_ref, kn_ref, vn_ref, o_ref):
    q_t = q_ref[0, 0, 0]
    T = q_t.shape[1]
    n_c, _, t_c = vc_ref.shape[2:]
    s_c = _dot(kc_ref[0, 0], q_t)
    s_n = _dot(kn_ref[0, 0], q_t)
    key = lax.broadcasted_iota(jnp.int32, (T, T), 0)
    qry = lax.broadcasted_iota(jnp.int32, (T, T), 1)
    s_n = jnp.where(key <= qry, s_n, NEG_INF)
    m = jnp.maximum(jnp.max(s_c, axis=0, keepdims=True), jnp.max(s_n, axis=0, keepdims=True))
    p_c = jnp.exp2(s_c - m).astype(BF16)
    acc = _dot(vn_ref[0, 0, 0], jnp.exp2(s_n - m).astype(BF16))
    for j in range(n_c):
        acc = acc + _dot(vc_ref[0, 0, j], p_c[j * t_c:(j + 1) * t_c])
    o_ref[0, 0, 0] = _finish_attention(acc)


def _sample_attention(qp, kp_cache, vp_cache, kp_new, vp_new):
    B, H, _, _, T = qp.shape
    P = kp_cache.shape[2]
    n_c, _, t_c = vp_cache.shape[2:]
    return pl.pallas_call(
        _sample_attn_kernel,
        grid=(B, H),
        in_specs=[
            pl.BlockSpec((1, 1, 1, AUG, T), lambda b, h: (b, h, 0, 0, 0)),
            pl.BlockSpec((1, 1, P, AUG), lambda b, h: (b, h, 0, 0)),
            pl.BlockSpec((1, 1, n_c, AUG, t_c), lambda b, h: (b, h, 0, 0, 0)),
            pl.BlockSpec((1, 1, T, AUG), lambda b, h: (b, h, 0, 0)),
            pl.BlockSpec((1, 1, 1, AUG, T), lambda b, h: (b, h, 0, 0, 0)),
        ],
        out_specs=pl.BlockSpec((1, 1, 1, FOX_DH, T), lambda b, h: (b, h, 0, 0, 0)),
        out_shape=jax.ShapeDtypeStruct((B, H, 1, FOX_DH, T), BF16),
        compiler_params=_params(("arbitrary", "arbitrary")),
        name="sample_attention",
    )(qp, kp_cache, vp_cache, kp_new, vp_new)


ATTN_UNROLL = 16
RET_TILE = 512
FFN_TILE = 512
FOX_TILE = 512
CACHE_TILE = 512


def kernel(x_prompt, x_sample, cache_k, cache_v, cache_logf, state_ret, state_ffn_conv,
           w_in_a, ln_ret_g, ln_ret_b, w_out_a, w_kvf, b_f, w_q_b, w_out_b,
           ln_mix_g, ln_mix_b, w_up, conv_w, conv_b, w_down, ln_ffn_g, ln_ffn_b):
    Bp, Lp, _ = x_prompt.shape
    Bs, Ls, _ = x_sample.shape
    past = cache_k.shape[1]
    H = FOX_HEADS

    w_in = w_in_a[0].astype(BF16)
    w_oa = w_out_a[0].astype(BF16)
    w_k = w_kvf[:, :HB].astype(BF16)
    w_vt = w_kvf[:, HB:2 * HB].T.astype(BF16)
    w_ft = w_kvf[:, 2 * HB:].T.astype(BF16)
    w_qt = (w_q_b[0] * (FOX_DH ** -0.5 * LOG2E)).T.astype(BF16)
    w_ob = w_out_b[0].astype(BF16)
    w_up_b = w_up.astype(BF16)
    w_dn_b = w_down.astype(BF16)

    def layer0(x, pos0, ret_state0, conv_state0, ret_tile, ffn_tile):
        x1, ret_new = _retention_layer(x, pos0, ret_state0, w_in, w_oa, ln_ret_g[0], ln_ret_b[0],
                                       ln_mix_g[0], ln_mix_b[0], ret_tile)
        x2, cs0 = _ffn_layer(x1, conv_state0, w_up_b[0], conv_w[0], conv_b[0], w_dn_b[0],
                             ln_ffn_g[0], ln_ffn_b[0], ffn_tile)
        return x2, ret_new, cs0

    def ffn1(x2, o_t, conv_state0, tile, n_valid=None):
        return _ffn_layer(x2, conv_state0, w_up_b[1], conv_w[1], conv_b[1], w_dn_b[1],
                          ln_ffn_g[1], ln_ffn_b[1], tile, n_valid=n_valid,
                          mix=(o_t, w_ob, ln_mix_g[1], ln_mix_b[1]))

    zeros_ret = jnp.zeros((Bp, RET_HEADS, RET_DK, RET_DV), F32)
    zeros_cs = jnp.zeros((Bp, CONV_W - 1, D_FF), F32)
    x2p, ret_p, cs0_p = layer0(x_prompt, 0, zeros_ret, zeros_cs, RET_TILE, FFN_TILE)
    kp_, vp_, lft_p, kop, vop, qop = _fox_projection(
        x2p, jnp.zeros((Bp, H, LANES), F32), w_k, w_vt, w_qt, w_ft, b_f, FOX_TILE)
    ot_p = _prompt_attention(qop, kop, vop, unroll=ATTN_UNROLL)
    y_p, cs1_p = ffn1(x2p, ot_p, zeros_cs, FFN_TILE)

    x2s, ret_s, cs0_s = layer0(x_sample, past, state_ret[0], state_ffn_conv[0], Ls, Ls)
    kc, vc, carry = _cache_operands(cache_k.reshape(Bs, past, HB), cache_v.reshape(Bs, past, HB),
                                    jnp.transpose(cache_logf.astype(F32), (0, 2, 1)), CACHE_TILE)
    Tpad = max(LANES, Ls)
    x2s_pad = jnp.pad(x2s, ((0, 0), (0, Tpad - Ls), (0, 0)))
    ks_, vs_, lft_s, kos, vos, qos = _fox_projection(x2s_pad, carry, w_k, w_vt, w_qt, w_ft, b_f, Tpad)
    ot_s = _sample_attention(qos, kc, vc, kos, vos)
    y_s_pad, cs1_s = ffn1(x2s_pad, ot_s, state_ffn_conv[1], Tpad, n_valid=Ls)

    def heads(a, L):
        return a[:, :L].reshape(a.shape[0], L, H, FOX_DH)

    return (y_p, y_s_pad[:, :Ls],
            ret_p[None], heads(kp_, Lp), heads(vp_, Lp), jnp.transpose(lft_p, (0, 2, 1)),
            jnp.stack([cs0_p, cs1_p]),
            ret_s[None], heads(ks_, Ls), heads(vs_, Ls), jnp.transpose(lft_s[:, :, :Ls], (0, 2, 1)),
            jnp.stack([cs0_s, cs1_s]))
```

```python
import functools
import math

import numpy as np
import jax
import jax.numpy as jnp
from jax import lax
from jax.experimental import pallas as pl
from jax.experimental.pallas import tpu as pltpu

F32 = jnp.float32
BF16 = jnp.bfloat16

D_MODEL = 1024
RET_DK = 256
RET_HEADS = D_MODEL // RET_DK
RET_DV = 2 * RET_DK
HK = RET_HEADS * RET_DK
HV = RET_HEADS * RET_DV
FOX_DH = 64
FOX_HEADS = D_MODEL // FOX_DH
HB = FOX_HEADS * FOX_DH
D_FF = 2816
CONV_W = 3
ROPE_BASE = 10000.0
LN_EPS = 1e-5
DEPTH = 2
ALPHA = (2 * DEPTH) ** 0.25
NEG_INF = -1e30
LOG2E = math.log2(math.e)

LANES = 128
AUG = 2 * FOX_DH
FF_CHUNK = 256
VMEM_LIMIT = 56 * 1024 * 1024

_COL_QC = (FOX_DH, FOX_DH + 1, FOX_DH + 2)
_COL_KC = (FOX_DH + 3, FOX_DH + 4, FOX_DH + 5)


def _dot(a, b):
    return jnp.dot(a, b, preferred_element_type=F32)


def _dot_nt(a, b):
    return lax.dot_general(a, b, (((1,), (1,)), ((), ())), preferred_element_type=F32)


def _dot_tn(a, b):
    return lax.dot_general(a, b, (((0,), (0,)), ((), ())), preferred_element_type=F32)


def _layernorm(z, g, b):
    mu = jnp.mean(z, axis=-1, keepdims=True)
    zc = z - mu
    var = jnp.mean(zc * zc, axis=-1, keepdims=True)
    return zc * lax.rsqrt(var + LN_EPS) * g + b


def _row_halves(n):
    if n >= 256 and n % 32 == 0:
        return [slice(0, n // 2), slice(n // 2, n)]
    return [slice(0, n)]


def _const_spec(shape):
    zeros = (0,) * len(shape)
    return pl.BlockSpec(shape, lambda *_: zeros, pipeline_mode=pl.Buffered(1))


def _params(sem, flags=None):
    return pltpu.CompilerParams(dimension_semantics=sem, vmem_limit_bytes=VMEM_LIMIT, flags=flags)


def _retention_kernel(x_ref, cos_ref, sin_ref, dmask_ref, qdec_ref, kdec_ref, s0_ref,
                      win_ref, wout_ref, lrg_ref, lrb_ref, lmg_ref, lmb_ref,
                      y_ref, s_ref, *, state_decay):
    @pl.when(pl.program_id(1) == 0)
    def _():
        s_ref[...] = s0_ref[...]

    x = x_ref[0]
    xb = x.astype(BF16)
    cos = cos_ref[...]
    sin = sin_ref[...]
    half = RET_DK // 2

    def rope(u):
        u1, u2 = u[:, :half], u[:, half:]
        return jnp.concatenate([u1 * cos - u2 * sin, u2 * cos + u1 * sin], axis=1)

    mix = None
    for h in range(RET_HEADS):
        q = rope(_dot(xb, win_ref[:, h * RET_DK:(h + 1) * RET_DK]))
        k = rope(_dot(xb, win_ref[:, HK + h * RET_DK:HK + (h + 1) * RET_DK])) * (RET_DK ** -0.5)
        v = _dot(xb, win_ref[:, 2 * HK + h * RET_DV:2 * HK + (h + 1) * RET_DV])
        g = _dot(xb, win_ref[:, 2 * HK + HV + h * RET_DV:2 * HK + HV + (h + 1) * RET_DV])
        vb = v.astype(BF16)
        scores = _dot_nt(q.astype(BF16), k.astype(BF16)) * dmask_ref[h]
        state = s_ref[0, h]
        o = _dot(scores.astype(BF16), vb) + _dot((q * qdec_ref[h]).astype(BF16), state.astype(BF16))
        s_ref[0, h] = state_decay[h] * state + _dot_tn((k * kdec_ref[h]).astype(BF16), vb)
        sl = slice(h * RET_DV, (h + 1) * RET_DV)
        o = _layernorm(o, lrg_ref[:, sl], lrb_ref[:, sl])
        gated = (jax.nn.silu(g) * o).astype(BF16)
        if h < RET_HEADS - 1:
            part = _dot(gated, wout_ref[sl, :])
            mix = part if mix is None else mix + part
        else:
            for rows in _row_halves(x.shape[0]):
                z = ALPHA * x[rows] + mix[rows] + _dot(gated[rows], wout_ref[sl, :])
                y_ref[0, rows, :] = _layernorm(z, lmg_ref[...], lmb_ref[...])


def _retention_layer(x, pos0, state0, w_in, w_out, ln_ret_g, ln_ret_b, ln_mix_g, ln_mix_b, tile):
    B, L, D = x.shape
    T = min(tile, L)
    assert L % T == 0
    nt = L // T
    pos = (pos0 + jnp.arange(L, dtype=jnp.int32)).astype(F32)
    half = RET_DK // 2
    inv = 1.0 / (ROPE_BASE ** (jnp.arange(half, dtype=F32) / half))
    ang = pos[:, None] * inv[None, :]
    cos, sin = jnp.cos(ang), jnp.sin(ang)
    lg = np.log1p(-np.exp2(-5.0 - np.arange(RET_HEADS, dtype=np.float64)))
    idx = np.arange(T, dtype=np.float64)
    diff = idx[:, None] - idx[None, :]
    dmask = np.where(diff >= 0, np.exp(lg[:, None, None] * np.maximum(diff, 0.0)), 0.0)
    qdec = np.exp((idx[None, :] + 1.0) * lg[:, None])[..., None]
    kdec = np.exp((T - 1.0 - idx)[None, :] * lg[:, None])[..., None]
    state_decay = tuple(float(v) for v in np.exp(T * lg))

    vec = lambda a: a.reshape(1, -1).astype(F32)
    kern = functools.partial(_retention_kernel, state_decay=state_decay)
    y, s_new = pl.pallas_call(
        kern,
        grid=(B, nt),
        in_specs=[
            pl.BlockSpec((1, T, D), lambda b, t: (b, t, 0)),
            pl.BlockSpec((T, half), lambda b, t: (t, 0)),
            pl.BlockSpec((T, half), lambda b, t: (t, 0)),
            _const_spec((RET_HEADS, T, T)),
            _const_spec((RET_HEADS, T, 1)),
            _const_spec((RET_HEADS, T, 1)),
            pl.BlockSpec((1, RET_HEADS, RET_DK, RET_DV), lambda b, t: (b, 0, 0, 0)),
            _const_spec(w_in.shape),
            _const_spec(w_out.shape),
            _const_spec((1, HV)), _const_spec((1, HV)),
            _const_spec((1, D)), _const_spec((1, D)),
        ],
        out_specs=[
            pl.BlockSpec((1, T, D), lambda b, t: (b, t, 0)),
            pl.BlockSpec((1, RET_HEADS, RET_DK, RET_DV), lambda b, t: (b, 0, 0, 0)),
        ],
        out_shape=[
            jax.ShapeDtypeStruct((B, L, D), F32),
            jax.ShapeDtypeStruct((B, RET_HEADS, RET_DK, RET_DV), F32),
        ],
        compiler_params=_params(("arbitrary", "arbitrary")),
        name="retention_layer",
    )(x, cos, sin, jnp.asarray(dmask, F32), jnp.asarray(qdec, F32), jnp.asarray(kdec, F32),
      state0, w_in, w_out, vec(ln_ret_g), vec(ln_ret_b), vec(ln_mix_g), vec(ln_mix_b))
    return y, s_new


def _ffn_kernel(*refs, n_valid, with_mix):
    if with_mix:
        (x_ref, ot_ref, wo_ref, lmg_ref, lmb_ref, cs0_ref, wup_ref, cw_ref, cb_ref, wdn_ref,
         lfg_ref, lfb_ref, y_ref, cs_ref, hid_ref) = refs
    else:
        (x_ref, cs0_ref, wup_ref, cw_ref, cb_ref, wdn_ref,
         lfg_ref, lfb_ref, y_ref, cs_ref, hid_ref) = refs

    @pl.when(pl.program_id(1) == 0)
    def _():
        cs_ref[...] = cs0_ref[...]

    x = x_ref[0]
    if with_mix:
        o_t = ot_ref[0, :, 0].reshape(HB, x.shape[0])
        x = jnp.concatenate(
            [_layernorm(ALPHA * x[rows] + _dot_tn(o_t[:, rows], wo_ref[...]), lmg_ref[...], lmb_ref[...])
             for rows in _row_halves(x.shape[0])], axis=0)
    xb = x.astype(BF16)
    T = x.shape[0]
    row = lax.broadcasted_iota(jnp.int32, (T, FF_CHUNK), 0)
    for c in range(D_FF // FF_CHUNK):
        sl = slice(c * FF_CHUNK, (c + 1) * FF_CHUNK)
        val = _dot(xb, wup_ref[:, sl])
        a = _dot(xb, wup_ref[:, D_FF + c * FF_CHUNK:D_FF + (c + 1) * FF_CHUNK])
        prev2 = cs_ref[0, 0:1, sl]
        prev1 = cs_ref[0, 1:2, sl]
        a1 = jnp.where(row == 0, prev1, pltpu.roll(a, 1, 0))
        a2 = jnp.where(row == 0, prev2, jnp.where(row == 1, prev1, pltpu.roll(a, 2, 0)))
        conv = cb_ref[:, sl] + cw_ref[0:1, sl] * a2 + cw_ref[1:2, sl] * a1 + cw_ref[2:3, sl] * a
        gelu = 0.5 * conv * (1.0 + lax.erf(conv * (2.0 ** -0.5)))
        hid_ref[:, sl] = (gelu * val).astype(BF16)
        cs_ref[0, :, sl] = a[n_valid - 2:n_valid, :]
    for rows in _row_halves(T):
        f = _dot(hid_ref[rows, :], wdn_ref[...])
        y_ref[0, rows, :] = _layernorm(ALPHA * x[rows] + f, lfg_ref[...], lfb_ref[...])


def _ffn_layer(x, conv_state0, w_up, conv_w, conv_b, w_down, ln_g, ln_b, tile, n_valid=None,
               mix=None):
    B, L, D = x.shape
    T = min(tile, L)
    assert L % T == 0
    nt = L // T
    if n_valid is None:
        n_valid = T
    else:
        assert nt == 1
    vec = lambda a: a.reshape(1, -1).astype(F32)
    row_spec = pl.BlockSpec((1, T, D), lambda b, t: (b, t, 0))
    cs_spec = pl.BlockSpec((1, CONV_W - 1, D_FF), lambda b, t: (b, 0, 0))
    args, specs = [x], [row_spec]
    if mix is not None:
        o_t, w_o, lmg, lmb = mix
        args += [o_t, w_o, vec(lmg), vec(lmb)]
        per_blk = o_t.shape[4] // T
        assert o_t.shape[4] % T == 0 and o_t.shape[2] * per_blk == nt
        specs += [pl.BlockSpec((1, FOX_HEADS, 1, FOX_DH, T),
                               lambda b, t: (b, 0, t // per_blk, 0, t % per_blk)),
                  _const_spec(w_o.shape),
                  _const_spec((1, D)), _const_spec((1, D))]
    args += [conv_state0, w_up, conv_w, vec(conv_b), w_down, vec(ln_g), vec(ln_b)]
    specs += [cs_spec, _const_spec(w_up.shape), _const_spec(conv_w.shape), _const_spec((1, D_FF)),
              _const_spec(w_down.shape), _const_spec((1, D)), _const_spec((1, D))]
    kern = functools.partial(_ffn_kernel, n_valid=n_valid, with_mix=mix is not None)
    y, cs = pl.pallas_call(
        kern,
        grid=(B, nt),
        in_specs=specs,
        out_specs=[row_spec, cs_spec],
        out_shape=[jax.ShapeDtypeStruct((B, L, D), F32),
                   jax.ShapeDtypeStruct((B, CONV_W - 1, D_FF), F32)],
        scratch_shapes=[pltpu.VMEM((T, D_FF), BF16)],
        compiler_params=_params(("arbitrary", "arbitrary")),
        name="ffn_mix_layer" if mix is not None else "ffn_layer",
    )(*args)
    return y, cs


def _cumsum_lanes(x):
    n = x.shape[1]
    lane = lax.broadcasted_iota(jnp.int32, x.shape, 1)
    s = 1
    while s < n:
        x = x + jnp.where(lane >= s, pltpu.roll(x, s, 1), 0.0)
        s *= 2
    return x


def _split3(c):
    hi = c.astype(BF16).astype(F32)
    r = c - hi
    mid = r.astype(BF16).astype(F32)
    lo = (r - mid).astype(BF16).astype(F32)
    return hi, mid, lo


def _advance_cumsum(logf_t, carry_ref):
    c_t = carry_ref[:, 0:1] + _cumsum_lanes(logf_t)
    T = logf_t.shape[1]
    carry_ref[...] = jnp.broadcast_to(c_t[:, T - 1:T], carry_ref.shape)
    return c_t


def _store_k_operand(k, c_parts, place_ref, kp_ref):
    hi, mid, lo = c_parts
    T = k.shape[0]
    cp_t = jnp.concatenate(
        [-hi, -mid, -lo, jnp.ones((8, T), F32), jnp.zeros((LANES - 3 * FOX_HEADS - 8, T), F32)], axis=0)
    cp = cp_t.T.astype(BF16)
    aug = _dot(cp, place_ref[...])
    lane = lax.broadcasted_iota(jnp.int32, (T, AUG), 1)
    for p in range(FOX_HEADS // 2):
        slab = k[:, p * AUG:(p + 1) * AUG]
        h0, h1 = 2 * p, 2 * p + 1
        kp_ref[0, h0] = jnp.where(lane < FOX_DH, slab, aug[:, h0 * AUG:(h0 + 1) * AUG]).astype(BF16)
        kp_ref[0, h1] = jnp.where(lane < FOX_DH, pltpu.roll(slab, FOX_DH, 1),
                                  aug[:, h1 * AUG:(h1 + 1) * AUG]).astype(BF16)


def _store_v_operand(v_t, vp_ref):
    T = v_t.shape[1]
    sub = lax.broadcasted_iota(jnp.int32, (FOX_DH, T), 0)
    ones_row = jnp.where(sub == 0, 1.0, 0.0).astype(BF16)
    for h in range(FOX_HEADS):
        vp_ref[0, h, 0, 0:FOX_DH, :] = v_t[h * FOX_DH:(h + 1) * FOX_DH].astype(BF16)
        vp_ref[0, h, 0, FOX_DH:AUG, :] = ones_row


def _store_q_operand(q_t, c_parts, qp_ref):
    hi, mid, lo = c_parts
    T = q_t.shape[1]
    sub = lax.broadcasted_iota(jnp.int32, (FOX_DH, T), 0)
    for h in range(FOX_HEADS):
        aug = jnp.where(sub == 0, hi[h:h + 1],
              jnp.where(sub == 1, mid[h:h + 1],
              jnp.where(sub == 2, lo[h:h + 1],
              jnp.where(sub < 6, 1.0, 0.0))))
        qp_ref[0, h, 0, 0:FOX_DH, :] = q_t[h * FOX_DH:(h + 1) * FOX_DH].astype(BF16)
        qp_ref[0, h, 0, FOX_DH:AUG, :] = aug.astype(BF16)


def _placement_matrix():
    p = np.zeros((LANES, FOX_HEADS * AUG), np.float32)
    for h in range(FOX_HEADS):
        for part in range(3):
            p[part * FOX_HEADS + h, h * AUG + _COL_KC[part]] = 1.0
            p[3 * FOX_HEADS, h * AUG + _COL_QC[part]] = 1.0
    return jnp.asarray(p, BF16)


def _fox_proj_kernel(x_ref, carry0_ref, wk_ref, wvt_ref, wqt_ref, wft_ref, bf_ref, place_ref,
                     k_ref, v_ref, lf_ref, kp_ref, vp_ref, qp_ref, carry_ref):
    @pl.when(pl.program_id(1) == 0)
    def _():
        carry_ref[...] = carry0_ref[0]

    xb = x_ref[0].astype(BF16)
    logf_t = jax.nn.log_sigmoid(_dot_nt(wft_ref[...], xb) + bf_ref[...])
    lf_ref[0] = logf_t
    c_parts = _split3(_advance_cumsum(logf_t, carry_ref) * LOG2E)
    v_t = _dot_nt(wvt_ref[...], xb)
    _store_v_operand(v_t, vp_ref)
    v_ref[0] = v_t.T.reshape(v_ref.shape[1:])
    _store_q_operand(_dot_nt(wqt_ref[...], xb), c_parts, qp_ref)
    k = _dot(xb, wk_ref[...])
    k_ref[0] = k.reshape(k_ref.shape[1:])
    _store_k_operand(k, c_parts, place_ref, kp_ref)


def _fox_projection(x, carry0, w_k, w_vt, w_qt, w_ft, b_f, tile):
    B, L, D = x.shape
    T = min(tile, L)
    assert L % T == 0 and T % LANES == 0
    nt = L // T
    H = FOX_HEADS
    return pl.pallas_call(
        _fox_proj_kernel,
        grid=(B, nt),
        in_specs=[
            pl.BlockSpec((1, T, D), lambda b, t: (b, t, 0)),
            pl.BlockSpec((1, H, LANES), lambda b, t: (b, 0, 0)),
            _const_spec(w_k.shape), _const_spec(w_vt.shape),
            _const_spec(w_qt.shape), _const_spec(w_ft.shape), _const_spec((H, 1)),
            _const_spec((LANES, H * AUG)),
        ],
        out_specs=[
            pl.BlockSpec((1, T, H, FOX_DH), lambda b, t: (b, t, 0, 0)),
            pl.BlockSpec((1, T, H, FOX_DH), lambda b, t: (b, t, 0, 0)),
            pl.BlockSpec((1, H, T), lambda b, t: (b, 0, t)),
            pl.BlockSpec((1, H, T, AUG), lambda b, t: (b, 0, t, 0)),
            pl.BlockSpec((1, H, 1, AUG, T), lambda b, t: (b, 0, t, 0, 0)),
            pl.BlockSpec((1, H, 1, AUG, T), lambda b, t: (b, 0, t, 0, 0)),
        ],
        out_shape=[
            jax.ShapeDtypeStruct((B, L, H, FOX_DH), F32),
            jax.ShapeDtypeStruct((B, L, H, FOX_DH), F32),
            jax.ShapeDtypeStruct((B, H, L), F32),
            jax.ShapeDtypeStruct((B, H, L, AUG), BF16),
            jax.ShapeDtypeStruct((B, H, nt, AUG, T), BF16),
            jax.ShapeDtypeStruct((B, H, nt, AUG, T), BF16),
        ],
        scratch_shapes=[pltpu.VMEM((H, LANES), F32)],
        compiler_params=_params(("arbitrary", "arbitrary")),
        name="fox_projection",
    )(x, carry0, w_k, w_vt, w_qt, w_ft, b_f.reshape(H, 1).astype(F32), _placement_matrix())


def _cache_kernel(k_ref, v_ref, lf_ref, place_ref, kp_ref, vp_ref, carry_ref):
    @pl.when(pl.program_id(1) == 0)
    def _():
        carry_ref[...] = jnp.zeros_like(carry_ref)

    c_parts = _split3(_advance_cumsum(lf_ref[0], carry_ref.at[0]) * LOG2E)
    _store_k_operand(k_ref[0], c_parts, place_ref, kp_ref)
    _store_v_operand(v_ref[0].T, vp_ref)


def _cache_operands(cache_k, cache_v, cache_logf_t, tile):
    B, P, _ = cache_k.shape
    T = min(tile, P)
    assert P % T == 0 and T % LANES == 0
    nt = P // T
    H = FOX_HEADS
    return pl.pallas_call(
        _cache_kernel,
        grid=(B, nt),
        in_specs=[
            pl.BlockSpec((1, T, HB), lambda b, t: (b, t, 0)),
            pl.BlockSpec((1, T, HB), lambda b, t: (b, t, 0)),
            pl.BlockSpec((1, H, T), lambda b, t: (b, 0, t)),
            _const_spec((LANES, H * AUG)),
        ],
        out_specs=[
            pl.BlockSpec((1, H, T, AUG), lambda b, t: (b, 0, t, 0)),
            pl.BlockSpec((1, H, 1, AUG, T), lambda b, t: (b, 0, t, 0, 0)),
            pl.BlockSpec((1, H, LANES), lambda b, t: (b, 0, 0)),
        ],
        out_shape=[
            jax.ShapeDtypeStruct((B, H, P, AUG), BF16),
            jax.ShapeDtypeStruct((B, H, nt, AUG, T), BF16),
            jax.ShapeDtypeStruct((B, H, LANES), F32),
        ],
        compiler_params=_params(("arbitrary", "arbitrary")),
        name="cache_operands",
    )(cache_k, cache_v, cache_logf_t, _placement_matrix())


def _finish_attention(acc):
    return (acc[0:FOX_DH] / acc[FOX_DH:FOX_DH + 1]).astype(BF16)


def _prompt_attn_kernel(ti_ref, tj_ref, q_ref, k_ref, v_ref, o_ref,
                        sa_ref, sb_ref, bma_ref, bmb_ref, m_ref, acc_ref,
                        *, n_masked, n_plain, unroll):
    nq, _, T = q_ref.shape[2:]

    def k_block(j):
        return k_ref[0, 0, pl.ds(pl.multiple_of(j * T, T), T), :]

    def init(i, carry):
        m_ref[i] = jnp.full((1, T), NEG_INF, F32)
        acc_ref[i] = jnp.zeros((AUG, T), F32)
        return carry

    lax.fori_loop(0, nq + 1, init, 0)

    bufs = ((sa_ref, bma_ref), (sb_ref, bmb_ref))

    def produce(t, slot, masked=False):
        s_ref, bm_ref = bufs[slot]
        k_blk = k_block(tj_ref[t])
        qi = jnp.minimum(ti_ref[t], nq - 1)
        s = _dot(k_blk, q_ref[0, 0, qi])
        if masked:
            key = lax.broadcasted_iota(jnp.int32, (T, T), 0)
            qry = lax.broadcasted_iota(jnp.int32, (T, T), 1)
            s = jnp.where(key <= qry, s, NEG_INF)
        s_ref[...] = s
        bm_ref[...] = jnp.max(s, axis=0, keepdims=True)

    def consume(t, slot):
        s_ref, bm_ref = bufs[slot]
        i = ti_ref[t]
        j = tj_ref[t]
        m_prev = m_ref[i]
        m_new = jnp.maximum(m_prev, bm_ref[...])
        p = jnp.exp2(s_ref[...] - m_new).astype(BF16)
        acc_ref[i] = jnp.exp2(m_prev - m_new) * acc_ref[i] + _dot(v_ref[0, 0, j], p)
        m_ref[i] = m_new

    produce(0, 0)

    def trip(u, carry, mask_last):
        t0 = unroll * u
        for k in range(unroll):
            produce(t0 + k + 1, (k + 1) % 2, masked=mask_last and k == unroll - 1)
            consume(t0 + k, k % 2)
        return carry

    lax.fori_loop(0, n_masked, functools.partial(trip, mask_last=True), 0)
    lax.fori_loop(n_masked, n_masked + n_plain, functools.partial(trip, mask_last=False), 0)

    def finish(i, carry):
        o_ref[0, 0, i] = _finish_attention(acc_ref[i])
        return carry

    lax.fori_loop(0, nq, finish, 0)


def _attention_tasks(nq, unroll):
    pad = (nq, 0)
    off = [(i, j) for i in range(nq) for j in range(i)]
    n_trips = max(nq + 1, -(-(len(off) + nq) // unroll))
    tasks = [None] * (n_trips * unroll + 1)
    for i in range(nq):
        tasks[unroll * (i + 1)] = (i, i)
    rest = iter(off)
    tasks = [t if t is not None else next(rest, pad) for t in tasks]
    ti = jnp.asarray([t[0] for t in tasks], jnp.int32)
    tj = jnp.asarray([t[1] for t in tasks], jnp.int32)
    return ti, tj, nq, n_trips - nq


def _prompt_attention(qp, kp, vp, *, unroll):
    nb, nh, nq, _, T = qp.shape
    L = nq * T
    assert unroll % 2 == 0
    ti, tj, n_masked, n_plain = _attention_tasks(nq, unroll)
    grid_spec = pltpu.PrefetchScalarGridSpec(
        num_scalar_prefetch=2,
        grid=(nb, nh),
        in_specs=[
            pl.BlockSpec((1, 1, nq, AUG, T), lambda b, h, ti, tj: (b, h, 0, 0, 0)),
            pl.BlockSpec((1, 1, L, AUG), lambda b, h, ti, tj: (b, h, 0, 0)),
            pl.BlockSpec((1, 1, nq, AUG, T), lambda b, h, ti, tj: (b, h, 0, 0, 0)),
        ],
        out_specs=pl.BlockSpec((1, 1, nq, FOX_DH, T), lambda b, h, ti, tj: (b, h, 0, 0, 0)),
        scratch_shapes=[
            pltpu.VMEM((T, T), F32), pltpu.VMEM((T, T), F32),
            pltpu.VMEM((1, T), F32), pltpu.VMEM((1, T), F32),
            pltpu.VMEM((nq + 1, 1, T), F32),
            pltpu.VMEM((nq + 1, AUG, T), F32),
        ],
    )
    kern = functools.partial(_prompt_attn_kernel, n_masked=n_masked, n_plain=n_plain,
                             unroll=unroll)
    return pl.pallas_call(
        kern,
        grid_spec=grid_spec,
        out_shape=jax.ShapeDtypeStruct((nb, nh, nq, FOX_DH, T), BF16),
        compiler_params=_params(("arbitrary", "arbitrary")),
        name="prompt_attention",
    )(ti, tj, qp, kp, vp)


def _sample_attn_kernel(q_ref, kc_ref, vc_ref, kn_ref, vn_ref, o_ref):
    q_t = q_ref[0, 0, 0]
    T = q_t.shape[1]
    n_c, _, t_c = vc_ref.shape[2:]
    s_c = _dot(kc_ref[0, 0], q_t)
    s_n = _dot(kn_ref[0, 0], q_t)
    key = lax.broadcasted_iota(jnp.int32, (T, T), 0)
    qry = lax.broadcasted_iota(jnp.int32, (T, T), 1)
    s_n = jnp.where(key <= qry, s_n, NEG_INF)
    m = jnp.maximum(jnp.max(s_c, axis=0, keepdims=True), jnp.max(s_n, axis=0, keepdims=True))
    p_c = jnp.exp2(s_c - m).astype(BF16)
    acc = _dot(vn_ref[0, 0, 0], jnp.exp2(s_n - m).astype(BF16))
    for j in range(n_c):
        acc = acc + _dot(vc_ref[0, 0, j], p_c[j * t_c:(j + 1) * t_c])
    o_ref[0, 0, 0] = _finish_attention(acc)


def _sample_attention(qp, kp_cache, vp_cache, kp_new, vp_new):
    B, H, _, _, T = qp.shape
    P = kp_cache.shape[2]
    n_c, _, t_c = vp_cache.shape[2:]
    return pl.pallas_call(
        _sample_attn_kernel,
        grid=(B, H),
        in_specs=[
            pl.BlockSpec((1, 1, 1, AUG, T), lambda b, h: (b, h, 0, 0, 0)),
            pl.BlockSpec((1, 1, P, AUG), lambda b, h: (b, h, 0, 0)),
            pl.BlockSpec((1, 1, n_c, AUG, t_c), lambda b, h: (b, h, 0, 0, 0)),
            pl.BlockSpec((1, 1, T, AUG), lambda b, h: (b, h, 0, 0)),
            pl.BlockSpec((1, 1, 1, AUG, T), lambda b, h: (b, h, 0, 0, 0)),
        ],
        out_specs=pl.BlockSpec((1, 1, 1, FOX_DH, T), lambda b, h: (b, h, 0, 0, 0)),
        out_shape=jax.ShapeDtypeStruct((B, H, 1, FOX_DH, T), BF16),
        compiler_params=_params(("arbitrary", "arbitrary")),
        name="sample_attention",
    )(qp, kp_cache, vp_cache, kp_new, vp_new)


ATTN_UNROLL = 16
RET_TILE = 512
FFN_TILE = 512
FOX_TILE = 512
CACHE_TILE = 512


def kernel(x_prompt, x_sample, cache_k, cache_v, cache_logf, state_ret, state_ffn_conv,
           w_in_a, ln_ret_g, ln_ret_b, w_out_a, w_kvf, b_f, w_q_b, w_out_b,
           ln_mix_g, ln_mix_b, w_up, conv_w, conv_b, w_down, ln_ffn_g, ln_ffn_b):
    Bp, Lp, _ = x_prompt.shape
    Bs, Ls, _ = x_sample.shape
    past = cache_k.shape[1]
    H = FOX_HEADS

    w_in = w_in_a[0].astype(BF16)
    w_oa = w_out_a[0].astype(BF16)
    w_k = w_kvf[:, :HB].astype(BF16)
    w_vt = w_kvf[:, HB:2 * HB].T.astype(BF16)
    w_ft = w_kvf[:, 2 * HB:].T.astype(BF16)
    w_qt = (w_q_b[0] * (FOX_DH ** -0.5 * LOG2E)).T.astype(BF16)
    w_ob = w_out_b[0].astype(BF16)
    w_up_b = w_up.astype(BF16)
    w_dn_b = w_down.astype(BF16)

    def layer0(x, pos0, ret_state0, conv_state0, ret_tile, ffn_tile):
        x1, ret_new = _retention_layer(x, pos0, ret_state0, w_in, w_oa, ln_ret_g[0], ln_ret_b[0],
                                       ln_mix_g[0], ln_mix_b[0], ret_tile)
        x2, cs0 = _ffn_layer(x1, conv_state0, w_up_b[0], conv_w[0], conv_b[0], w_dn_b[0],
                             ln_ffn_g[0], ln_ffn_b[0], ffn_tile)
        return x2, ret_new, cs0

    def ffn1(x2, o_t, conv_state0, tile, n_valid=None):
        return _ffn_layer(x2, conv_state0, w_up_b[1], conv_w[1], conv_b[1], w_dn_b[1],
                          ln_ffn_g[1], ln_ffn_b[1], tile, n_valid=n_valid,
                          mix=(o_t, w_ob, ln_mix_g[1], ln_mix_b[1]))

    zeros_ret = jnp.zeros((Bp, RET_HEADS, RET_DK, RET_DV), F32)
    zeros_cs = jnp.zeros((Bp, CONV_W - 1, D_FF), F32)
    x2p, ret_p, cs0_p = layer0(x_prompt, 0, zeros_ret, zeros_cs, RET_TILE, FFN_TILE)
    kp_, vp_, lft_p, kop, vop, qop = _fox_projection(
        x2p, jnp.zeros((Bp, H, LANES), F32), w_k, w_vt, w_qt, w_ft, b_f, FOX_TILE)
    ot_p = _prompt_attention(qop, kop, vop, unroll=ATTN_UNROLL)
    y_p, cs1_p = ffn1(x2p, ot_p, zeros_cs, FFN_TILE)

    x2s, ret_s, cs0_s = layer0(x_sample, past, state_ret[0], state_ffn_conv[0], Ls, Ls)
    kc, vc, carry = _cache_operands(cache_k.reshape(Bs, past, HB), cache_v.reshape(Bs, past, HB),
                                    jnp.transpose(cache_logf.astype(F32), (0, 2, 1)), CACHE_TILE)
    Tpad = max(LANES, Ls)
    x2s_pad = jnp.pad(x2s, ((0, 0), (0, Tpad - Ls), (0, 0)))
    ks_, vs_, lft_s, kos, vos, qos = _fox_projection(x2s_pad, carry, w_k, w_vt, w_qt, w_ft, b_f, Tpad)
    ot_s = _sample_attention(qos, kc, vc, kos, vos)
    y_s_pad, cs1_s = ffn1(x2s_pad, ot_s, state_ffn_conv[1], Tpad, n_valid=Ls)

    def heads(a, L):
        return a[:, :L]

    return (y_p, y_s_pad[:, :Ls],
            ret_p[None], heads(kp_, Lp), heads(vp_, Lp), jnp.transpose(lft_p, (0, 2, 1)),
            jnp.stack([cs0_p, cs1_p]),
            ret_s[None], heads(ks_, Ls), heads(vs_, Ls), jnp.transpose(lft_s[:, :, :Ls], (0, 2, 1)),
            jnp.stack([cs0_s, cs1_s]))
```

```python
import functools
import math

import numpy as np
import jax
import jax.numpy as jnp
from jax import lax
from jax.experimental import pallas as pl
from jax.experimental.pallas import tpu as pltpu

F32 = jnp.float32
BF16 = jnp.bfloat16

D_MODEL = 1024
RET_DK = 256
RET_HEADS = D_MODEL // RET_DK
RET_DV = 2 * RET_DK
HK = RET_HEADS * RET_DK
HV = RET_HEADS * RET_DV
FOX_DH = 64
FOX_HEADS = D_MODEL // FOX_DH
HB = FOX_HEADS * FOX_DH
D_FF = 2816
CONV_W = 3
ROPE_BASE = 10000.0
LN_EPS = 1e-5
DEPTH = 2
ALPHA = (2 * DEPTH) ** 0.25
NEG_INF = -1e30
LOG2E = math.log2(math.e)

LANES = 128
AUG = 2 * FOX_DH
FF_CHUNK = 256
VMEM_LIMIT = 56 * 1024 * 1024

_COL_QC = (FOX_DH, FOX_DH + 1, FOX_DH + 2)
_COL_KC = (FOX_DH + 3, FOX_DH + 4, FOX_DH + 5)


def _dot(a, b):
    return jnp.dot(a, b, preferred_element_type=F32)


def _dot_nt(a, b):
    return lax.dot_general(a, b, (((1,), (1,)), ((), ())), preferred_element_type=F32)


def _dot_tn(a, b):
    return lax.dot_general(a, b, (((0,), (0,)), ((), ())), preferred_element_type=F32)


def _layernorm(z, g, b):
    mu = jnp.mean(z, axis=-1, keepdims=True)
    zc = z - mu
    var = jnp.mean(zc * zc, axis=-1, keepdims=True)
    return zc * lax.rsqrt(var + LN_EPS) * g + b


def _row_halves(n):
    if n >= 256 and n % 32 == 0:
        return [slice(0, n // 2), slice(n // 2, n)]
    return [slice(0, n)]


def _const_spec(shape):
    zeros = (0,) * len(shape)
    return pl.BlockSpec(shape, lambda *_: zeros, pipeline_mode=pl.Buffered(1))


def _params(sem, flags=None):
    return pltpu.CompilerParams(dimension_semantics=sem, vmem_limit_bytes=VMEM_LIMIT, flags=flags)


def _retention_kernel(x_ref, cos_ref, sin_ref, dmask_ref, qdec_ref, kdec_ref, s0_ref,
                      win_ref, wout_ref, lrg_ref, lrb_ref, lmg_ref, lmb_ref,
                      y_ref, s_ref, *, state_decay):
    @pl.when(pl.program_id(1) == 0)
    def _():
        s_ref[...] = s0_ref[...]

    x = x_ref[0]
    xb = x.astype(BF16)
    cos = cos_ref[...]
    sin = sin_ref[...]
    half = RET_DK // 2

    def rope(u):
        u1, u2 = u[:, :half], u[:, half:]
        return jnp.concatenate([u1 * cos - u2 * sin, u2 * cos + u1 * sin], axis=1)

    mix = None
    for h in range(RET_HEADS):
        q = rope(_dot(xb, win_ref[:, h * RET_DK:(h + 1) * RET_DK]))
        k = rope(_dot(xb, win_ref[:, HK + h * RET_DK:HK + (h + 1) * RET_DK])) * (RET_DK ** -0.5)
        v = _dot(xb, win_ref[:, 2 * HK + h * RET_DV:2 * HK + (h + 1) * RET_DV])
        g = _dot(xb, win_ref[:, 2 * HK + HV + h * RET_DV:2 * HK + HV + (h + 1) * RET_DV])
        vb = v.astype(BF16)
        scores = _dot_nt(q.astype(BF16), k.astype(BF16)) * dmask_ref[h]
        state = s_ref[0, h]
        o = _dot(scores.astype(BF16), vb) + _dot((q * qdec_ref[h]).astype(BF16), state.astype(BF16))
        s_ref[0, h] = state_decay[h] * state + _dot_tn((k * kdec_ref[h]).astype(BF16), vb)
        sl = slice(h * RET_DV, (h + 1) * RET_DV)
        o = _layernorm(o, lrg_ref[:, sl], lrb_ref[:, sl])
        gated = (jax.nn.silu(g) * o).astype(BF16)
        if h < RET_HEADS - 1:
            part = _dot(gated, wout_ref[sl, :])
            mix = part if mix is None else mix + part
        else:
            for rows in _row_halves(x.shape[0]):
                z = ALPHA * x[rows] + mix[rows] + _dot(gated[rows], wout_ref[sl, :])
                y_ref[0, rows, :] = _layernorm(z, lmg_ref[...], lmb_ref[...])


def _retention_layer(x, pos0, state0, w_in, w_out, ln_ret_g, ln_ret_b, ln_mix_g, ln_mix_b, tile):
    B, L, D = x.shape
    T = min(tile, L)
    assert L % T == 0
    nt = L // T
    pos = (pos0 + jnp.arange(L, dtype=jnp.int32)).astype(F32)
    half = RET_DK // 2
    inv = 1.0 / (ROPE_BASE ** (jnp.arange(half, dtype=F32) / half))
    ang = pos[:, None] * inv[None, :]
    cos, sin = jnp.cos(ang), jnp.sin(ang)
    lg = np.log1p(-np.exp2(-5.0 - np.arange(RET_HEADS, dtype=np.float64)))
    idx = np.arange(T, dtype=np.float64)
    diff = idx[:, None] - idx[None, :]
    dmask = np.where(diff >= 0, np.exp(lg[:, None, None] * np.maximum(diff, 0.0)), 0.0)
    qdec = np.exp((idx[None, :] + 1.0) * lg[:, None])[..., None]
    kdec = np.exp((T - 1.0 - idx)[None, :] * lg[:, None])[..., None]
    state_decay = tuple(float(v) for v in np.exp(T * lg))

    vec = lambda a: a.reshape(1, -1).astype(F32)
    kern = functools.partial(_retention_kernel, state_decay=state_decay)
    y, s_new = pl.pallas_call(
        kern,
        grid=(B, nt),
        in_specs=[
            pl.BlockSpec((1, T, D), lambda b, t: (b, t, 0)),
            pl.BlockSpec((T, half), lambda b, t: (t, 0)),
            pl.BlockSpec((T, half), lambda b, t: (t, 0)),
            _const_spec((RET_HEADS, T, T)),
            _const_spec((RET_HEADS, T, 1)),
            _const_spec((RET_HEADS, T, 1)),
            pl.BlockSpec((1, RET_HEADS, RET_DK, RET_DV), lambda b, t: (b, 0, 0, 0)),
            _const_spec(w_in.shape),
            _const_spec(w_out.shape),
            _const_spec((1, HV)), _const_spec((1, HV)),
            _const_spec((1, D)), _const_spec((1, D)),
        ],
        out_specs=[
            pl.BlockSpec((1, T, D), lambda b, t: (b, t, 0)),
            pl.BlockSpec((1, RET_HEADS, RET_DK, RET_DV), lambda b, t: (b, 0, 0, 0)),
        ],
        out_shape=[
            jax.ShapeDtypeStruct((B, L, D), F32),
            jax.ShapeDtypeStruct((B, RET_HEADS, RET_DK, RET_DV), F32),
        ],
        compiler_params=_params(("arbitrary", "arbitrary")),
        name="retention_layer",
    )(x, cos, sin, jnp.asarray(dmask, F32), jnp.asarray(qdec, F32), jnp.asarray(kdec, F32),
      state0, w_in, w_out, vec(ln_ret_g), vec(ln_ret_b), vec(ln_mix_g), vec(ln_mix_b))
    return y, s_new


def _ffn_kernel(*refs, n_valid, with_mix):
    if with_mix:
        (x_ref, ot_ref, wo_ref, lmg_ref, lmb_ref, cs0_ref, wup_ref, cw_ref, cb_ref, wdn_ref,
         lfg_ref, lfb_ref, y_ref, cs_ref, hid_ref) = refs
    else:
        (x_ref, cs0_ref, wup_ref, cw_ref, cb_ref, wdn_ref,
         lfg_ref, lfb_ref, y_ref, cs_ref, hid_ref) = refs

    @pl.when(pl.program_id(1) == 0)
    def _():
        cs_ref[...] = cs0_ref[...]

    x = x_ref[0]
    if with_mix:
        o_t = ot_ref[0, :, 0].reshape(HB, x.shape[0])
        x = jnp.concatenate(
            [_layernorm(ALPHA * x[rows] + _dot_tn(o_t[:, rows], wo_ref[...]), lmg_ref[...], lmb_ref[...])
             for rows in _row_halves(x.shape[0])], axis=0)
    xb = x.astype(BF16)
    T = x.shape[0]
    row = lax.broadcasted_iota(jnp.int32, (T, FF_CHUNK), 0)
    for c in range(D_FF // FF_CHUNK):
        sl = slice(c * FF_CHUNK, (c + 1) * FF_CHUNK)
        val = _dot(xb, wup_ref[:, sl])
        a = _dot(xb, wup_ref[:, D_FF + c * FF_CHUNK:D_FF + (c + 1) * FF_CHUNK])
        prev2 = cs_ref[0, 0:1, sl]
        prev1 = cs_ref[0, 1:2, sl]
        a1 = jnp.where(row == 0, prev1, pltpu.roll(a, 1, 0))
        a2 = jnp.where(row == 0, prev2, jnp.where(row == 1, prev1, pltpu.roll(a, 2, 0)))
        conv = cb_ref[:, sl] + cw_ref[0:1, sl] * a2 + cw_ref[1:2, sl] * a1 + cw_ref[2:3, sl] * a
        gelu = 0.5 * conv * (1.0 + lax.erf(conv * (2.0 ** -0.5)))
        hid_ref[:, sl] = (gelu * val).astype(BF16)
        cs_ref[0, :, sl] = a[n_valid - 2:n_valid, :]
    for rows in _row_halves(T):
        f = _dot(hid_ref[rows, :], wdn_ref[...])
        y_ref[0, rows, :] = _layernorm(ALPHA * x[rows] + f, lfg_ref[...], lfb_ref[...])


def _ffn_layer(x, conv_state0, w_up, conv_w, conv_b, w_down, ln_g, ln_b, tile, n_valid=None,
               mix=None):
    B, L, D = x.shape
    T = min(tile, L)
    assert L % T == 0
    nt = L // T
    if n_valid is None:
        n_valid = T
    else:
        assert nt == 1
    vec = lambda a: a.reshape(1, -1).astype(F32)
    row_spec = pl.BlockSpec((1, T, D), lambda b, t: (b, t, 0))
    cs_spec = pl.BlockSpec((1, CONV_W - 1, D_FF), lambda b, t: (b, 0, 0))
    args, specs = [x], [row_spec]
    if mix is not None:
        o_t, w_o, lmg, lmb = mix
        args += [o_t, w_o, vec(lmg), vec(lmb)]
        per_blk = o_t.shape[4] // T
        assert o_t.shape[4] % T == 0 and o_t.shape[2] * per_blk == nt
        specs += [pl.BlockSpec((1, FOX_HEADS, 1, FOX_DH, T),
                               lambda b, t: (b, 0, t // per_blk, 0, t % per_blk)),
                  _const_spec(w_o.shape),
                  _const_spec((1, D)), _const_spec((1, D))]
    args += [conv_state0, w_up, conv_w, vec(conv_b), w_down, vec(ln_g), vec(ln_b)]
    specs += [cs_spec, _const_spec(w_up.shape), _const_spec(conv_w.shape), _const_spec((1, D_FF)),
              _const_spec(w_down.shape), _const_spec((1, D)), _const_spec((1, D))]
    kern = functools.partial(_ffn_kernel, n_valid=n_valid, with_mix=mix is not None)
    y, cs = pl.pallas_call(
        kern,
        grid=(B, nt),
        in_specs=specs,
        out_specs=[row_spec, cs_spec],
        out_shape=[jax.ShapeDtypeStruct((B, L, D), F32),
                   jax.ShapeDtypeStruct((B, CONV_W - 1, D_FF), F32)],
        scratch_shapes=[pltpu.VMEM((T, D_FF), BF16)],
        compiler_params=_params(("arbitrary", "arbitrary")),
        name="ffn_mix_layer" if mix is not None else "ffn_layer",
    )(*args)
    return y, cs


def _cumsum_lanes(x):
    n = x.shape[1]
    lane = lax.broadcasted_iota(jnp.int32, x.shape, 1)
    s = 1
    while s < n:
        x = x + jnp.where(lane >= s, pltpu.roll(x, s, 1), 0.0)
        s *= 2
    return x


def _split3(c):
    hi = c.astype(BF16).astype(F32)
    r = c - hi
    mid = r.astype(BF16).astype(F32)
    lo = (r - mid).astype(BF16).astype(F32)
    return hi, mid, lo


def _advance_cumsum(logf_t, carry_ref):
    c_t = carry_ref[:, 0:1] + _cumsum_lanes(logf_t)
    T = logf_t.shape[1]
    carry_ref[...] = jnp.broadcast_to(c_t[:, T - 1:T], carry_ref.shape)
    return c_t


def _store_k_operand(k, c_parts, place_ref, kp_ref):
    hi, mid, lo = c_parts
    T = k.shape[0]
    cp_t = jnp.concatenate(
        [-hi, -mid, -lo, jnp.ones((8, T), F32), jnp.zeros((LANES - 3 * FOX_HEADS - 8, T), F32)], axis=0)
    cp = cp_t.T.astype(BF16)
    aug = _dot(cp, place_ref[...])
    lane = lax.broadcasted_iota(jnp.int32, (T, AUG), 1)
    for p in range(FOX_HEADS // 2):
        slab = k[:, p * AUG:(p + 1) * AUG]
        h0, h1 = 2 * p, 2 * p + 1
        kp_ref[0, h0] = jnp.where(lane < FOX_DH, slab, aug[:, h0 * AUG:(h0 + 1) * AUG]).astype(BF16)
        kp_ref[0, h1] = jnp.where(lane < FOX_DH, pltpu.roll(slab, FOX_DH, 1),
                                  aug[:, h1 * AUG:(h1 + 1) * AUG]).astype(BF16)


def _store_v_operand(v_t, vp_ref):
    T = v_t.shape[1]
    sub = lax.broadcasted_iota(jnp.int32, (FOX_DH, T), 0)
    ones_row = jnp.where(sub == 0, 1.0, 0.0).astype(BF16)
    for h in range(FOX_HEADS):
        vp_ref[0, h, 0, 0:FOX_DH, :] = v_t[h * FOX_DH:(h + 1) * FOX_DH].astype(BF16)
        vp_ref[0, h, 0, FOX_DH:AUG, :] = ones_row


def _store_q_operand(q_t, c_parts, qp_ref):
    hi, mid, lo = c_parts
    T = q_t.shape[1]
    sub = lax.broadcasted_iota(jnp.int32, (FOX_DH, T), 0)
    for h in range(FOX_HEADS):
        aug = jnp.where(sub == 0, hi[h:h + 1],
              jnp.where(sub == 1, mid[h:h + 1],
              jnp.where(sub == 2, lo[h:h + 1],
              jnp.where(sub < 6, 1.0, 0.0))))
        qp_ref[0, h, 0, 0:FOX_DH, :] = q_t[h * FOX_DH:(h + 1) * FOX_DH].astype(BF16)
        qp_ref[0, h, 0, FOX_DH:AUG, :] = aug.astype(BF16)


def _placement_matrix():
    p = np.zeros((LANES, FOX_HEADS * AUG), np.float32)
    for h in range(FOX_HEADS):
        for part in range(3):
            p[part * FOX_HEADS + h, h * AUG + _COL_KC[part]] = 1.0
            p[3 * FOX_HEADS, h * AUG + _COL_QC[part]] = 1.0
    return jnp.asarray(p, BF16)


def _fox_proj_kernel(x_ref, carry0_ref, wk_ref, wvt_ref, wqt_ref, wft_ref, bf_ref, place_ref,
                     k_ref, v_ref, lf_ref, kp_ref, vp_ref, qp_ref, carry_ref):
    @pl.when(pl.program_id(1) == 0)
    def _():
        carry_ref[...] = carry0_ref[0]

    xb = x_ref[0].astype(BF16)
    logf_t = jax.nn.log_sigmoid(_dot_nt(wft_ref[...], xb) + bf_ref[...])
    lf_ref[0] = logf_t
    c_parts = _split3(_advance_cumsum(logf_t, carry_ref) * LOG2E)
    v_t = _dot_nt(wvt_ref[...], xb)
    _store_v_operand(v_t, vp_ref)
    v_ref[0] = v_t.T
    _store_q_operand(_dot_nt(wqt_ref[...], xb), c_parts, qp_ref)
    k = _dot(xb, wk_ref[...])
    k_ref[0] = k
    _store_k_operand(k, c_parts, place_ref, kp_ref)


def _fox_projection(x, carry0, w_k, w_vt, w_qt, w_ft, b_f, tile):
    B, L, D = x.shape
    T = min(tile, L)
    assert L % T == 0 and T % LANES == 0
    nt = L // T
    H = FOX_HEADS
    return pl.pallas_call(
        _fox_proj_kernel,
        grid=(B, nt),
        in_specs=[
            pl.BlockSpec((1, T, D), lambda b, t: (b, t, 0)),
            pl.BlockSpec((1, H, LANES), lambda b, t: (b, 0, 0)),
            _const_spec(w_k.shape), _const_spec(w_vt.shape),
            _const_spec(w_qt.shape), _const_spec(w_ft.shape), _const_spec((H, 1)),
            _const_spec((LANES, H * AUG)),
        ],
        out_specs=[
            pl.BlockSpec((1, T, HB), lambda b, t: (b, t, 0)),
            pl.BlockSpec((1, T, HB), lambda b, t: (b, t, 0)),
            pl.BlockSpec((1, H, T), lambda b, t: (b, 0, t)),
            pl.BlockSpec((1, H, T, AUG), lambda b, t: (b, 0, t, 0)),
            pl.BlockSpec((1, H, 1, AUG, T), lambda b, t: (b, 0, t, 0, 0)),
            pl.BlockSpec((1, H, 1, AUG, T), lambda b, t: (b, 0, t, 0, 0)),
        ],
        out_shape=[
            jax.ShapeDtypeStruct((B, L, HB), F32),
            jax.ShapeDtypeStruct((B, L, HB), F32),
            jax.ShapeDtypeStruct((B, H, L), F32),
            jax.ShapeDtypeStruct((B, H, L, AUG), BF16),
            jax.ShapeDtypeStruct((B, H, nt, AUG, T), BF16),
            jax.ShapeDtypeStruct((B, H, nt, AUG, T), BF16),
        ],
        scratch_shapes=[pltpu.VMEM((H, LANES), F32)],
        compiler_params=_params(("arbitrary", "arbitrary")),
        name="fox_projection",
    )(x, carry0, w_k, w_vt, w_qt, w_ft, b_f.reshape(H, 1).astype(F32), _placement_matrix())


def _cache_kernel(k_ref, v_ref, lf_ref, place_ref, kp_ref, vp_ref, carry_ref):
    @pl.when(pl.program_id(1) == 0)
    def _():
        carry_ref[...] = jnp.zeros_like(carry_ref)

    c_parts = _split3(_advance_cumsum(lf_ref[0], carry_ref.at[0]) * LOG2E)
    _store_k_operand(k_ref[0], c_parts, place_ref, kp_ref)
    _store_v_operand(v_ref[0].T, vp_ref)


def _cache_operands(cache_k, cache_v, cache_logf_t, tile):
    B, P, _ = cache_k.shape
    T = min(tile, P)
    assert P % T == 0 and T % LANES == 0
    nt = P // T
    H = FOX_HEADS
    return pl.pallas_call(
        _cache_kernel,
        grid=(B, nt),
        in_specs=[
            pl.BlockSpec((1, T, HB), lambda b, t: (b, t, 0)),
            pl.BlockSpec((1, T, HB), lambda b, t: (b, t, 0)),
            pl.BlockSpec((1, H, T), lambda b, t: (b, 0, t)),
            _const_spec((LANES, H * AUG)),
        ],
        out_specs=[
            pl.BlockSpec((1, H, T, AUG), lambda b, t: (b, 0, t, 0)),
            pl.BlockSpec((1, H, 1, AUG, T), lambda b, t: (b, 0, t, 0, 0)),
            pl.BlockSpec((1, H, LANES), lambda b, t: (b, 0, 0)),
        ],
        out_shape=[
            jax.ShapeDtypeStruct((B, H, P, AUG), BF16),
            jax.ShapeDtypeStruct((B, H, nt, AUG, T), BF16),
            jax.ShapeDtypeStruct((B, H, LANES), F32),
        ],
        compiler_params=_params(("arbitrary", "arbitrary")),
        name="cache_operands",
    )(cache_k, cache_v, cache_logf_t, _placement_matrix())


def _finish_attention(acc):
    return (acc[0:FOX_DH] / acc[FOX_DH:FOX_DH + 1]).astype(BF16)


def _prompt_attn_kernel(ti_ref, tj_ref, q_ref, k_ref, v_ref, o_ref,
                        sa_ref, sb_ref, bma_ref, bmb_ref, m_ref, acc_ref,
                        *, n_masked, n_plain, unroll):
    nq, _, T = q_ref.shape[2:]

    def k_block(j):
        return k_ref[0, 0, pl.ds(pl.multiple_of(j * T, T), T), :]

    def init(i, carry):
        m_ref[i] = jnp.full((1, T), NEG_INF, F32)
        acc_ref[i] = jnp.zeros((AUG, T), F32)
        return carry

    lax.fori_loop(0, nq + 1, init, 0)

    bufs = ((sa_ref, bma_ref), (sb_ref, bmb_ref))

    def produce(t, slot, masked=False):
        s_ref, bm_ref = bufs[slot]
        k_blk = k_block(tj_ref[t])
        qi = jnp.minimum(ti_ref[t], nq - 1)
        s = _dot(k_blk, q_ref[0, 0, qi])
        if masked:
            key = lax.broadcasted_iota(jnp.int32, (T, T), 0)
            qry = lax.broadcasted_iota(jnp.int32, (T, T), 1)
            s = jnp.where(key <= qry, s, NEG_INF)
        s_ref[...] = s
        bm_ref[0:1, :] = jnp.max(s, axis=0, keepdims=True)

    def consume(t, slot):
        s_ref, bm_ref = bufs[slot]
        i = ti_ref[t]
        j = tj_ref[t]
        m_prev = m_ref[i]
        m_new = jnp.maximum(m_prev, bm_ref[0:1, :])
        p = jnp.exp2(s_ref[...] - m_new).astype(BF16)
        acc_ref[i] = jnp.exp2(m_prev - m_new) * acc_ref[i] + _dot(v_ref[0, 0, j], p)
        m_ref[i] = m_new

    produce(0, 0)

    def trip(u, carry, mask_last):
        t0 = unroll * u
        for k in range(unroll):
            produce(t0 + k + 1, (k + 1) % 2, masked=mask_last and k == unroll - 1)
            consume(t0 + k, k % 2)
        return carry

    lax.fori_loop(0, n_masked, functools.partial(trip, mask_last=True), 0)
    lax.fori_loop(n_masked, n_masked + n_plain, functools.partial(trip, mask_last=False), 0)

    def finish(i, carry):
        o_ref[0, 0, i] = _finish_attention(acc_ref[i])
        return carry

    lax.fori_loop(0, nq, finish, 0)


def _attention_tasks(nq, unroll):
    pad = (nq, 0)
    off = [(i, j) for i in range(nq) for j in range(i)]
    n_trips = max(nq + 1, -(-(len(off) + nq) // unroll))
    tasks = [None] * (n_trips * unroll + 1)
    for i in range(nq):
        tasks[unroll * (i + 1)] = (i, i)
    rest = iter(off)
    tasks = [t if t is not None else next(rest, pad) for t in tasks]
    ti = jnp.asarray([t[0] for t in tasks], jnp.int32)
    tj = jnp.asarray([t[1] for t in tasks], jnp.int32)
    return ti, tj, nq, n_trips - nq


def _prompt_attention(qp, kp, vp, *, unroll):
    nb, nh, nq, _, T = qp.shape
    L = nq * T
    assert unroll % 2 == 0
    ti, tj, n_masked, n_plain = _attention_tasks(nq, unroll)
    grid_spec = pltpu.PrefetchScalarGridSpec(
        num_scalar_prefetch=2,
        grid=(nb, nh),
        in_specs=[
            pl.BlockSpec((1, 1, nq, AUG, T), lambda b, h, ti, tj: (b, h, 0, 0, 0)),
            pl.BlockSpec((1, 1, L, AUG), lambda b, h, ti, tj: (b, h, 0, 0)),
            pl.BlockSpec((1, 1, nq, AUG, T), lambda b, h, ti, tj: (b, h, 0, 0, 0)),
        ],
        out_specs=pl.BlockSpec((1, 1, nq, FOX_DH, T), lambda b, h, ti, tj: (b, h, 0, 0, 0)),
        scratch_shapes=[
            pltpu.VMEM((T, T), F32), pltpu.VMEM((T, T), F32),
            pltpu.VMEM((8, T), F32), pltpu.VMEM((8, T), F32),
            pltpu.VMEM((-(-(nq + 1) // 8) * 8, 1, T), F32),
            pltpu.VMEM((nq + 1, AUG, T), F32),
        ],
    )
    kern = functools.partial(_prompt_attn_kernel, n_masked=n_masked, n_plain=n_plain,
                             unroll=unroll)
    return pl.pallas_call(
        kern,
        grid_spec=grid_spec,
        out_shape=jax.ShapeDtypeStruct((nb, nh, nq, FOX_DH, T), BF16),
        compiler_params=_params(("arbitrary", "arbitrary")),
        name="prompt_attention",
    )(ti, tj, qp, kp, vp)


def _sample_attn_kernel(q_ref, kc_ref, vc_ref, kn_ref, vn_ref, o_ref):
    q_t = q_ref[0, 0, 0]
    T = q_t.shape[1]
    n_c, _, t_c = vc_ref.shape[2:]
    s_c = _dot(kc_ref[0, 0], q_t)
    s_n = _dot(kn_ref[0, 0], q_t)
    key = lax.broadcasted_iota(jnp.int32, (T, T), 0)
    qry = lax.broadcasted_iota(jnp.int32, (T, T), 1)
    s_n = jnp.where(key <= qry, s_n, NEG_INF)
    m = jnp.maximum(jnp.max(s_c, axis=0, keepdims=True), jnp.max(s_n, axis=0, keepdims=True))
    p_c = jnp.exp2(s_c - m).astype(BF16)
    acc = _dot(vn_ref[0, 0, 0], jnp.exp2(s_n - m).astype(BF16))
    for j in range(n_c):
        acc = acc + _dot(vc_ref[0, 0, j], p_c[j * t_c:(j + 1) * t_c])
    o_ref[0, 0, 0] = _finish_attention(acc)


def _sample_attention(qp, kp_cache, vp_cache, kp_new, vp_new):
    B, H, _, _, T = qp.shape
    P = kp_cache.shape[2]
    n_c, _, t_c = vp_cache.shape[2:]
    return pl.pallas_call(
        _sample_attn_kernel,
        grid=(B, H),
        in_specs=[
            pl.BlockSpec((1, 1, 1, AUG, T), lambda b, h: (b, h, 0, 0, 0)),
            pl.BlockSpec((1, 1, P, AUG), lambda b, h: (b, h, 0, 0)),
            pl.BlockSpec((1, 1, n_c, AUG, t_c), lambda b, h: (b, h, 0, 0, 0)),
            pl.BlockSpec((1, 1, T, AUG), lambda b, h: (b, h, 0, 0)),
            pl.BlockSpec((1, 1, 1, AUG, T), lambda b, h: (b, h, 0, 0, 0)),
        ],
        out_specs=pl.BlockSpec((1, 1, 1, FOX_DH, T), lambda b, h: (b, h, 0, 0, 0)),
        out_shape=jax.ShapeDtypeStruct((B, H, 1, FOX_DH, T), BF16),
        compiler_params=_params(("arbitrary", "arbitrary")),
        name="sample_attention",
    )(qp, kp_cache, vp_cache, kp_new, vp_new)


ATTN_UNROLL = 16
RET_TILE = 512
FFN_TILE = 512
FOX_TILE = 512
CACHE_TILE = 512


def kernel(x_prompt, x_sample, cache_k, cache_v, cache_logf, state_ret, state_ffn_conv,
           w_in_a, ln_ret_g, ln_ret_b, w_out_a, w_kvf, b_f, w_q_b, w_out_b,
           ln_mix_g, ln_mix_b, w_up, conv_w, conv_b, w_down, ln_ffn_g, ln_ffn_b):
    Bp, Lp, _ = x_prompt.shape
    Bs, Ls, _ = x_sample.shape
    past = cache_k.shape[1]
    H = FOX_HEADS

    w_in = w_in_a[0].astype(BF16)
    w_oa = w_out_a[0].astype(BF16)
    w_k = w_kvf[:, :HB].astype(BF16)
    w_vt = w_kvf[:, HB:2 * HB].T.astype(BF16)
    w_ft = w_kvf[:, 2 * HB:].T.astype(BF16)
    w_qt = (w_q_b[0] * (FOX_DH ** -0.5 * LOG2E)).T.astype(BF16)
    w_ob = w_out_b[0].astype(BF16)
    w_up_b = w_up.astype(BF16)
    w_dn_b = w_down.astype(BF16)

    def layer0(x, pos0, ret_state0, conv_state0, ret_tile, ffn_tile):
        x1, ret_new = _retention_layer(x, pos0, ret_state0, w_in, w_oa, ln_ret_g[0], ln_ret_b[0],
                                       ln_mix_g[0], ln_mix_b[0], ret_tile)
        x2, cs0 = _ffn_layer(x1, conv_state0, w_up_b[0], conv_w[0], conv_b[0], w_dn_b[0],
                             ln_ffn_g[0], ln_ffn_b[0], ffn_tile)
        return x2, ret_new, cs0

    def ffn1(x2, o_t, conv_state0, tile, n_valid=None):
        return _ffn_layer(x2, conv_state0, w_up_b[1], conv_w[1], conv_b[1], w_dn_b[1],
                          ln_ffn_g[1], ln_ffn_b[1], tile, n_valid=n_valid,
                          mix=(o_t, w_ob, ln_mix_g[1], ln_mix_b[1]))

    zeros_ret = jnp.zeros((Bp, RET_HEADS, RET_DK, RET_DV), F32)
    zeros_cs = jnp.zeros((Bp, CONV_W - 1, D_FF), F32)
    x2p, ret_p, cs0_p = layer0(x_prompt, 0, zeros_ret, zeros_cs, RET_TILE, FFN_TILE)
    kp_, vp_, lft_p, kop, vop, qop = _fox_projection(
        x2p, jnp.zeros((Bp, H, LANES), F32), w_k, w_vt, w_qt, w_ft, b_f, FOX_TILE)
    ot_p = _prompt_attention(qop, kop, vop, unroll=ATTN_UNROLL)
    y_p, cs1_p = ffn1(x2p, ot_p, zeros_cs, FFN_TILE)

    x2s, ret_s, cs0_s = layer0(x_sample, past, state_ret[0], state_ffn_conv[0], Ls, Ls)
    kc, vc, carry = _cache_operands(cache_k.reshape(Bs, past, HB), cache_v.reshape(Bs, past, HB),
                                    jnp.transpose(cache_logf.astype(F32), (0, 2, 1)), CACHE_TILE)
    Tpad = max(LANES, Ls)
    x2s_pad = jnp.pad(x2s, ((0, 0), (0, Tpad - Ls), (0, 0)))
    ks_, vs_, lft_s, kos, vos, qos = _fox_projection(x2s_pad, carry, w_k, w_vt, w_qt, w_ft, b_f, Tpad)
    ot_s = _sample_attention(qos, kc, vc, kos, vos)
    y_s_pad, cs1_s = ffn1(x2s_pad, ot_s, state_ffn_conv[1], Tpad, n_valid=Ls)

    def heads(a, L):
        return a[:, :L].reshape(a.shape[0], L, H, FOX_DH)

    return (y_p, y_s_pad[:, :Ls],
            ret_p[None], heads(kp_, Lp), heads(vp_, Lp), jnp.transpose(lft_p, (0, 2, 1)),
            jnp.stack([cs0_p, cs1_p]),
            ret_s[None], heads(ks_, Ls), heads(vs_, Ls), jnp.transpose(lft_s[:, :, :Ls], (0, 2, 1)),
            jnp.stack([cs0_s, cs1_s]))
```
